```python
import math
import jax, jax.numpy as jnp
from jax import lax
import numpy as np

D_MODEL = 2048
BATCH = 2
SEQ = 4096
DEPTH = 4
DEC_BATCH = 128
DEC_SEQ = 1
PAST_LEN = 8192
PAGE_SIZE = 128

N_BRANCH = 4
MIX_W = D_MODEL // 4
PLE_DIM = 256
EPS = 1e-6
Q_BLOCK = 128

CONV_CH = MIX_W
CONV_WIDTH = 31

MLA_HEADS = 8
MLA_NOPE = 64
MLA_ROPE = 32
MLA_QK = MLA_NOPE + MLA_ROPE
MLA_V = MIX_W // MLA_HEADS
MLA_Q_LORA = D_MODEL // 4
MLA_KV_LORA = D_MODEL // 8
MLA_SCALE = MLA_QK ** -0.5
ROPE_BASE = 10000.0

SB_HEADS = 8
SB_KV_HEADS = 2
SB_REP = SB_HEADS // SB_KV_HEADS
SB_HEAD_DIM = MIX_W // SB_HEADS
SB_SCALE = SB_HEAD_DIM ** -0.5

GMLP_CH = MIX_W
GMLP_GROUPS = 4
GMLP_GROUP_CH = GMLP_CH // GMLP_GROUPS
CHUNK = 128

N_GROUPS = 4
EXPERTS_PER_GROUP = 8
N_EXPERTS = N_GROUPS * EXPERTS_PER_GROUP
TOP_K = 2
D_EXPERT = D_MODEL // 8

IN_SIZES = (
    ('conv', 2 * CONV_CH),
    ('mla_q', MLA_Q_LORA),
    ('mla_kv', MLA_KV_LORA),
    ('mla_kr', MLA_ROPE),
    ('sb_q', SB_HEADS * SB_HEAD_DIM),
    ('sb_k', SB_KV_HEADS * SB_HEAD_DIM),
    ('sb_v', SB_KV_HEADS * SB_HEAD_DIM),
    ('gmlp', 2 * GMLP_CH),
    ('gate', N_BRANCH * D_MODEL),
)
IN_TOTAL = sum(n for _, n in IN_SIZES)

kernel_name = 'hybrid_gated_mla_stickbreak_conv_gmlp_hmoe_step'


def _rms(x, g):
    xf = x.astype(jnp.float32)
    y = xf * lax.rsqrt(jnp.mean(xf * xf, -1, keepdims=True) + EPS)
    return (y * g).astype(x.dtype)


def _ln(x, g, b):
    xf = x.astype(jnp.float32)
    xc = xf - jnp.mean(xf, -1, keepdims=True)
    y = xc * lax.rsqrt(jnp.mean(xc * xc, -1, keepdims=True) + EPS)
    return (y * g + b).astype(x.dtype)


def _split(z):
    out, o = {}, 0
    for name, n in IN_SIZES:
        out[name] = z[..., o:o + n]
        o += n
    return out


def _rope(x, pos):
    half = x.shape[-1] // 2
    inv = ROPE_BASE ** (-jnp.arange(half, dtype=jnp.float32) / half)
    ang = pos.astype(jnp.float32)[:, None] * inv[None, :]
    ang = ang.reshape((ang.shape[0],) + (1,) * (x.ndim - 3) + (half,))
    cos, sin = jnp.cos(ang), jnp.sin(ang)
    x1 = x[..., :half].astype(jnp.float32)
    x2 = x[..., half:].astype(jnp.float32)
    return jnp.concatenate([x1 * cos - x2 * sin, x1 * sin + x2 * cos], -1).astype(x.dtype)


def _glu(za):
    a, b = jnp.split(za, 2, axis=-1)
    return a * jax.nn.sigmoid(b)


def _conv_tail(a_ext, lp):
    y = lax.conv_general_dilated(
        a_ext, lp['conv_w'][:, None, :].astype(a_ext.dtype), window_strides=(1,), padding='VALID',
        dimension_numbers=('NWC', 'WIO', 'NWC'), feature_group_count=CONV_CH)
    y = y + lp['conv_b']
    return jax.nn.silu(_ln(y, lp['conv_ln_g'], lp['conv_ln_b']))


def _mla_project(s, pos, lp):
    n, l = s['mla_q'].shape[:2]
    c_q = _rms(s['mla_q'], lp['mla_q_norm_g'])
    q = (c_q @ lp['mla_w_uq']).reshape(n, l, MLA_HEADS, MLA_QK)
    q = jnp.concatenate([q[..., :MLA_NOPE], _rope(q[..., MLA_NOPE:], pos)], -1)
    q = _rms(q, lp['mla_qk_norm_q'])
    c_kv = _rms(s['mla_kv'], lp['mla_kv_norm_g'])
    k_pe = _rope(s['mla_kr'], pos)
    return q, c_kv, k_pe


def _mla_keys(c_kv, k_pe, lp):
    k_nope = (c_kv @ lp['mla_w_uk']).reshape(c_kv.shape[:-1] + (MLA_HEADS, MLA_NOPE))
    k_rot = jnp.broadcast_to(k_pe[..., None, :], k_nope.shape[:-1] + (MLA_ROPE,))
    return _rms(jnp.concatenate([k_nope, k_rot], -1), lp['mla_qk_norm_k'])


def _mla_values(o_lat, lp):
    w_uv = lp['mla_w_uv'].reshape(MLA_KV_LORA, MLA_HEADS, MLA_V)
    o = jnp.einsum('...hc,chd->...hd', o_lat, w_uv)
    return o.reshape(o.shape[:-2] + (MLA_HEADS * MLA_V,))


def _mla_prompt(q, c_kv, k_pe, lp):
    n, l = q.shape[:2]
    k = _mla_keys(c_kv, k_pe, lp)
    nb = l // Q_BLOCK
    qb = q.reshape(n, nb, Q_BLOCK, MLA_HEADS, MLA_QK).swapaxes(0, 1)
    kpos = jnp.arange(l)

    def block(args):
        i, qi = args
        sc = jnp.einsum('bqhd,bkhd->bhqk', qi, k).astype(jnp.float32) * MLA_SCALE
        qpos = i * Q_BLOCK + jnp.arange(Q_BLOCK)
        sc = jnp.where(kpos[None, :] <= qpos[:, None], sc, -jnp.inf)
        p = jax.nn.softmax(sc, axis=-1).astype(c_kv.dtype)
        return jnp.einsum('bhqk,bkc->bqhc', p, c_kv)

    o_lat = lax.map(block, (jnp.arange(nb), qb))
    o_lat = o_lat.swapaxes(0, 1).reshape(n, l, MLA_HEADS, MLA_KV_LORA)
    return _mla_values(o_lat, lp)


def _mla_sample(q, c_new, kpe_new, cache_ckv, cache_kpe, page_table, layer, lp):
    t_new = q.shape[1]

    def one(args):
        qb, cb, kb, pt = args
        c_all = jnp.concatenate([cache_ckv[layer, pt].reshape(-1, MLA_KV_LORA).astype(cb.dtype), cb], 0)
        kpe_all = jnp.concatenate([cache_kpe[layer, pt].reshape(-1, MLA_ROPE).astype(kb.dtype), kb], 0)
        past = c_all.shape[0] - t_new
        k = _mla_keys(c_all, kpe_all, lp)
        sc = jnp.einsum('thd,khd->htk', qb, k).astype(jnp.float32) * MLA_SCALE
        allow = jnp.arange(past + t_new)[None, :] <= past + jnp.arange(t_new)[:, None]
        p = jax.nn.softmax(jnp.where(allow, sc, -jnp.inf), axis=-1).astype(c_all.dtype)
        return jnp.einsum('htk,kc->thc', p, c_all)

    return _mla_values(lax.map(one, (q, c_new, kpe_new, page_table)), lp)


def _sb_weights(z, allow):
    log_beta = jax.nn.log_sigmoid(z)
    log_keep = jnp.where(allow, jax.nn.log_sigmoid(-z), 0.0)
    later = lax.cumsum(log_keep, axis=z.ndim - 1, reverse=True) - log_keep
    return jnp.where(allow, jnp.exp(log_beta + later), 0.0)


def _sb_prompt(q, k, v):
    n, l = q.shape[:2]
    nb = l // Q_BLOCK
    qb = q.reshape(n, nb, Q_BLOCK, SB_KV_HEADS, SB_REP, SB_HEAD_DIM).swapaxes(0, 1)
    kpos = jnp.arange(l)

    def block(args):
        i, qi = args
        z = jnp.einsum('bqgrd,bkgd->bgrqk', qi, k).astype(jnp.float32) * SB_SCALE
        qpos = i * Q_BLOCK + jnp.arange(Q_BLOCK)
        a = _sb_weights(z, kpos[None, :] < qpos[:, None])
        return jnp.einsum('bgrqk,bkgd->bqgrd', a.astype(v.dtype), v)

    o = lax.map(block, (jnp.arange(nb), qb))
    return o.swapaxes(0, 1).reshape(n, l, SB_HEADS * SB_HEAD_DIM)


def _sb_sample(q, k_new, v_new, cache_k, cache_v, page_table, layer):
    t_new = q.shape[1]

    def one(args):
        qb, kb, vb, pt = args
        k_all = jnp.concatenate([cache_k[layer, pt].reshape(-1, SB_KV_HEADS, SB_HEAD_DIM).astype(kb.dtype), kb], 0)
        v_all = jnp.concatenate([cache_v[layer, pt].reshape(-1, SB_KV_HEADS, SB_HEAD_DIM).astype(vb.dtype), vb], 0)
        past = k_all.shape[0] - t_new
        z = jnp.einsum('tgrd,kgd->grtk', qb, k_all).astype(jnp.float32) * SB_SCALE
        allow = jnp.arange(past + t_new)[None, :] < past + jnp.arange(t_new)[:, None]
        a = _sb_weights(z, allow)
        return jnp.einsum('grtk,kgd->tgrd', a.astype(vb.dtype), v_all)

    o = lax.map(one, (q, k_new, v_new, page_table))
    return o.reshape(o.shape[0], t_new, SB_HEADS * SB_HEAD_DIM)


def _gmlp_uv(zg, lp):
    u, v = jnp.split(jax.nn.gelu(zg), 2, axis=-1)
    return u, _ln(v, lp['gmlp_ln_g'], lp['gmlp_ln_b'])


def _gmlp_prompt(u, v, lp):
    n, l = u.shape[:2]
    vc = v.reshape(n, l // CHUNK, CHUNK, GMLP_GROUPS, GMLP_GROUP_CH)
    w = jnp.where(jnp.tril(jnp.ones((CHUNK, CHUNK), bool)), lp['gmlp_ws'], 0.0).astype(v.dtype)
    mix = jnp.einsum('gts,bnsgc->bntgc', w, vc) + lp['gmlp_b'].T[:, :, None].astype(v.dtype)
    return u * mix.reshape(n, l, GMLP_CH)


def _gmlp_sample(u, v, pos, lp):
    n, t_new = u.shape[:2]
    pc, cid, idx = pos % CHUNK, pos // CHUNK, jnp.arange(t_new)
    allow = (cid[:, None] == cid[None, :]) & (idx[None, :] <= idx[:, None])
    w = jnp.where(allow, lp['gmlp_ws'][:, pc[:, None], pc[None, :]], 0.0).astype(v.dtype)
    mix = jnp.einsum('gts,bsgc->btgc', w, v.reshape(n, t_new, GMLP_GROUPS, GMLP_GROUP_CH))
    mix = mix + lp['gmlp_b'][:, pc].T[:, :, None].astype(v.dtype)
    return u * mix.reshape(n, t_new, GMLP_CH)


def _moe(h, lp):
    shp = h.shape
    x = h.reshape(-1, D_MODEL)
    t = x.shape[0]
    g_logits = (x @ lp['router_group_w']).astype(jnp.float32) + lp['router_group_b']
    g_p, g_idx = lax.top_k(jax.nn.softmax(g_logits, axis=-1), 1)
    e_logits = ((x @ lp['router_expert_w']).astype(jnp.float32) + lp['router_expert_b']).reshape(t, N_GROUPS, EXPERTS_PER_GROUP)
    e_sel = jnp.take_along_axis(e_logits, g_idx[:, :, None], axis=1)[:, 0]
    e_val, e_idx = lax.top_k(e_sel, TOP_K)
    w = jax.nn.softmax(e_val, axis=-1) * g_p
    eid = g_idx * EXPERTS_PER_GROUP + e_idx
    combine = jnp.sum(jax.nn.one_hot(eid, N_EXPERTS, dtype=jnp.float32) * w[..., None], axis=1)
    hg = jnp.einsum('td,edf->tef', x, lp['moe_w_gate'])
    hu = jnp.einsum('td,edf->tef', x, lp['moe_w_up'])
    hid = jax.nn.silu(hg) * hu * combine[:, :, None].astype(x.dtype)
    return jnp.einsum('tef,efd->td', hid, lp['moe_w_down']).reshape(shp)


def _mixers_prompt(z, pos, lp):
    s = _split(z)
    n, l = z.shape[:2]
    a = _glu(s['conv'])
    out_a = _conv_tail(jnp.pad(a, ((0, 0), (CONV_WIDTH - 1, 0), (0, 0))), lp)
    q, c_kv, k_pe = _mla_project(s, pos, lp)
    out_b = _mla_prompt(q, c_kv, k_pe, lp)
    qs = s['sb_q'].reshape(n, l, SB_KV_HEADS, SB_REP, SB_HEAD_DIM)
    ks = s['sb_k'].reshape(n, l, SB_KV_HEADS, SB_HEAD_DIM)
    vs = s['sb_v'].reshape(n, l, SB_KV_HEADS, SB_HEAD_DIM)
    out_c = _sb_prompt(qs, ks, vs)
    u, v = _gmlp_uv(s['gmlp'], lp)
    out_d = _gmlp_prompt(u, v, lp)
    return (out_a, out_b, out_c, out_d), (a[:, l - (CONV_WIDTH - 1):], c_kv, k_pe, ks, vs)


def _mixers_sample(z, pos, conv_buf, cache_ckv, cache_kpe, cache_k, cache_v, page_table, layer, lp):
    s = _split(z)
    n, t_new = z.shape[:2]
    a = _glu(s['conv'])
    a_ext = jnp.concatenate([conv_buf.astype(a.dtype), a], axis=1)
    out_a = _conv_tail(a_ext, lp)
    q, c_kv, k_pe = _mla_project(s, pos, lp)
    out_b = _mla_sample(q, c_kv, k_pe, cache_ckv, cache_kpe, page_table, layer, lp)
    qs = s['sb_q'].reshape(n, t_new, SB_KV_HEADS, SB_REP, SB_HEAD_DIM)
    ks = s['sb_k'].reshape(n, t_new, SB_KV_HEADS, SB_HEAD_DIM)
    vs = s['sb_v'].reshape(n, t_new, SB_KV_HEADS, SB_HEAD_DIM)
    out_c = _sb_sample(qs, ks, vs, cache_k, cache_v, page_table, layer)
    u, v = _gmlp_uv(s['gmlp'], lp)
    out_d = _gmlp_sample(u, v, pos, lp)
    return (out_a, out_b, out_c, out_d), (a_ext[:, t_new:], c_kv, k_pe, ks, vs, v)


def _finish_layer(r, z, branches, p_l, lp):
    gates = jax.nn.sigmoid(_split(z)['gate']).reshape(z.shape[:-1] + (N_BRANCH, D_MODEL))
    proj = jnp.einsum('...nm,nmd->...nd', jnp.stack(branches, axis=-2), lp['w_branch'])
    r = r + jnp.sum(gates * proj, axis=-2) @ lp['w_out']
    r = r + _moe(_rms(r, lp['norm_ffn_g']), lp)
    gate = jax.nn.sigmoid(_rms(r, lp['ple_norm_g']) @ lp['ple_w_gate'])
    return r + gate * (p_l @ lp['ple_w_proj'])


def setup_inputs(seed: int = 0) -> dict:
    key = jax.random.key(seed)
    keys = iter(jax.random.split(key, 48))
    f32 = jnp.float32
    n_pages = PAST_LEN // PAGE_SIZE
    n_pool = (DEC_BATCH * n_pages * 5) // 4

    def nrm(shape, scale):
        return jax.random.normal(next(keys), shape, f32) * scale

    def gain(shape):
        return 1.0 + 0.02 * jax.random.normal(next(keys), shape, f32)

    perm = jax.random.permutation(next(keys), n_pool)
    page_table = perm[:DEC_BATCH * n_pages].reshape(DEC_BATCH, n_pages).astype(jnp.int32)
    d = D_MODEL
    return {
        'x_prompt': nrm((BATCH, SEQ, d), 1.0),
        'x_sample': nrm((DEC_BATCH, DEC_SEQ, d), 1.0),
        'cache_mla_ckv': nrm((DEPTH, n_pool, PAGE_SIZE, MLA_KV_LORA), 1.0),
        'cache_mla_kpe': nrm((DEPTH, n_pool, PAGE_SIZE, MLA_ROPE), 1.0),
        'cache_sb_k': nrm((DEPTH, n_pool, PAGE_SIZE, SB_KV_HEADS, SB_HEAD_DIM), 1.0),
        'cache_sb_v': nrm((DEPTH, n_pool, PAGE_SIZE, SB_KV_HEADS, SB_HEAD_DIM), 1.0),
        'state_conv': nrm((DEPTH, DEC_BATCH, CONV_WIDTH - 1, CONV_CH), 0.5),
        'page_table': page_table,
        'p_prompt': nrm((DEPTH, BATCH, SEQ, PLE_DIM), 1.0),
        'p_sample': nrm((DEPTH, DEC_BATCH, DEC_SEQ, PLE_DIM), 1.0),
        'norm_mix_g': gain((DEPTH, d)),
        'w_in': nrm((DEPTH, d, IN_TOTAL), d ** -0.5),
        'conv_w': nrm((DEPTH, CONV_WIDTH, CONV_CH), CONV_WIDTH ** -0.5),
        'conv_b': nrm((DEPTH, CONV_CH), 0.02),
        'conv_ln_g': gain((DEPTH, CONV_CH)),
        'conv_ln_b': nrm((DEPTH, CONV_CH), 0.02),
        'mla_q_norm_g': gain((DEPTH, MLA_Q_LORA)),
        'mla_kv_norm_g': gain((DEPTH, MLA_KV_LORA)),
        'mla_w_uq': nrm((DEPTH, MLA_Q_LORA, MLA_HEADS * MLA_QK), MLA_Q_LORA ** -0.5),
        'mla_w_uk': nrm((DEPTH, MLA_KV_LORA, MLA_HEADS * MLA_NOPE), MLA_KV_LORA ** -0.5),
        'mla_w_uv': nrm((DEPTH, MLA_KV_LORA, MLA_HEADS * MLA_V), MLA_KV_LORA ** -0.5),
        'mla_qk_norm_q': gain((DEPTH, MLA_QK)),
        'mla_qk_norm_k': gain((DEPTH, MLA_QK)),
        'gmlp_ln_g': gain((DEPTH, GMLP_CH)),
        'gmlp_ln_b': nrm((DEPTH, GMLP_CH), 0.02),
        'gmlp_ws': nrm((DEPTH, GMLP_GROUPS, CHUNK, CHUNK), 0.5 * CHUNK ** -0.5),
        'gmlp_b': gain((DEPTH, GMLP_GROUPS, CHUNK)),
        'w_branch': nrm((DEPTH, N_BRANCH, MIX_W, d), MIX_W ** -0.5),
        'w_out': nrm((DEPTH, d, d), d ** -0.5),
        'norm_ffn_g': gain((DEPTH, d)),
        'router_group_w': nrm((DEPTH, d, N_GROUPS), d ** -0.5),
        'router_group_b': nrm((DEPTH, N_GROUPS), 0.01),
        'router_expert_w': nrm((DEPTH, d, N_EXPERTS), d ** -0.5),
        'router_expert_b': nrm((DEPTH, N_EXPERTS), 0.01),
        'moe_w_gate': nrm((DEPTH, N_EXPERTS, d, D_EXPERT), d ** -0.5),
        'moe_w_up': nrm((DEPTH, N_EXPERTS, d, D_EXPERT), d ** -0.5),
        'moe_w_down': nrm((DEPTH, N_EXPERTS, D_EXPERT, d), D_EXPERT ** -0.5),
        'ple_norm_g': gain((DEPTH, d)),
        'ple_w_gate': nrm((DEPTH, d, d), d ** -0.5),
        'ple_w_proj': nrm((DEPTH, PLE_DIM, d), PLE_DIM ** -0.5),
    }


def reference(x_prompt, x_sample, cache_mla_ckv, cache_mla_kpe, cache_sb_k, cache_sb_v, state_conv, page_table,
              p_prompt, p_sample, norm_mix_g, w_in, conv_w, conv_b, conv_ln_g, conv_ln_b, mla_q_norm_g,
              mla_kv_norm_g, mla_w_uq, mla_w_uk, mla_w_uv, mla_qk_norm_q, mla_qk_norm_k, gmlp_ln_g, gmlp_ln_b,
              gmlp_ws, gmlp_b, w_branch, w_out, norm_ffn_g, router_group_w, router_group_b, router_expert_w,
              router_expert_b, moe_w_gate, moe_w_up, moe_w_down, ple_norm_g, ple_w_gate, ple_w_proj):
    past_len = page_table.shape[1] * cache_mla_ckv.shape[2]
    pos_p = jnp.arange(x_prompt.shape[1])
    pos_s = past_len + jnp.arange(x_sample.shape[1])
    rp, rs = x_prompt, x_sample
    st_p = [[] for _ in range(5)]
    st_s = [[] for _ in range(6)]
    for l in range(DEPTH):
        lp = dict(conv_w=conv_w[l], conv_b=conv_b[l], conv_ln_g=conv_ln_g[l], conv_ln_b=conv_ln_b[l],
                  mla_q_norm_g=mla_q_norm_g[l], mla_kv_norm_g=mla_kv_norm_g[l], mla_w_uq=mla_w_uq[l],
                  mla_w_uk=mla_w_uk[l], mla_w_uv=mla_w_uv[l], mla_qk_norm_q=mla_qk_norm_q[l],
                  mla_qk_norm_k=mla_qk_norm_k[l], gmlp_ln_g=gmlp_ln_g[l], gmlp_ln_b=gmlp_ln_b[l],
                  gmlp_ws=gmlp_ws[l], gmlp_b=gmlp_b[l], w_branch=w_branch[l], w_out=w_out[l],
                  norm_ffn_g=norm_ffn_g[l], router_group_w=router_group_w[l], router_group_b=router_group_b[l],
                  router_expert_w=router_expert_w[l], router_expert_b=router_expert_b[l],
                  moe_w_gate=moe_w_gate[l], moe_w_up=moe_w_up[l], moe_w_down=moe_w_down[l],
                  ple_norm_g=ple_norm_g[l], ple_w_gate=ple_w_gate[l], ple_w_proj=ple_w_proj[l])
        zp = _rms(rp, norm_mix_g[l]) @ w_in[l]
        br_p, new_p = _mixers_prompt(zp, pos_p, lp)
        rp = _finish_layer(rp, zp, br_p, p_prompt[l], lp)
        zs = _rms(rs, norm_mix_g[l]) @ w_in[l]
        br_s, new_s = _mixers_sample(zs, pos_s, state_conv[l], cache_mla_ckv, cache_mla_kpe, cache_sb_k,
                                     cache_sb_v, page_table, l, lp)
        rs = _finish_layer(rs, zs, br_s, p_sample[l], lp)
        for j in range(5):
            st_p[j].append(new_p[j])
        for j in range(6):
            st_s[j].append(new_s[j])
    conv_state_prompt, mla_ckv_prompt, mla_kpe_prompt, sb_k_prompt, sb_v_prompt = [jnp.stack(a) for a in st_p]
    conv_state_sample, mla_ckv_sample, mla_kpe_sample, sb_k_sample, sb_v_sample, gmlp_v_sample = [jnp.stack(a) for a in st_s]
    return (rp, rs, conv_state_prompt, conv_state_sample, mla_ckv_prompt, mla_ckv_sample, mla_kpe_prompt,
            mla_kpe_sample, sb_k_prompt, sb_k_sample, sb_v_prompt, sb_v_sample, gmlp_v_sample)
```

```python
import functools
import math

import numpy as np
import jax
import jax.numpy as jnp
from jax import lax
from jax.experimental import pallas as pl
from jax.experimental.pallas import tpu as pltpu

F32 = jnp.float32
BF16 = jnp.bfloat16

D_MODEL = 2048
N_BRANCH = 4
MIX_W = 512
PLE_DIM = 256
EPS = 1e-6
CONV_WIDTH = 31
MLA_HEADS = 8
MLA_NOPE = 64
MLA_ROPE = 32
MLA_QK = MLA_NOPE + MLA_ROPE
MLA_V = 64
MLA_Q_LORA = 512
MLA_KV_LORA = 256
MLA_SCALE = MLA_QK ** -0.5
ROPE_BASE = 10000.0
SB_HEADS = 8
SB_KV_HEADS = 2
SB_REP = SB_HEADS // SB_KV_HEADS
SB_HEAD_DIM = 64
SB_SCALE = SB_HEAD_DIM ** -0.5
GMLP_GROUPS = 4
CHUNK = 128
N_GROUPS = 4
EXPERTS_PER_GROUP = 8
N_EXPERTS = N_GROUPS * EXPERTS_PER_GROUP
D_EXPERT = 256

LANES = 128
HEAD_PAD = LANES
NEG_BIG = -1e30
SB_DEAD = -104.0

Z_CONV, Z_GMLP, Z_MQ, Z_SQ, Z_MKV, Z_SK, Z_SV, Z_KR = 0, 1024, 2048, 2560, 3072, 3328, 3456, 3584
Z_W = 3840
KR_LANE = 64

IN_SIZES = (('conv', 1024), ('mla_q', 512), ('mla_kv', 256), ('mla_kr', 32), ('sb_q', 512), ('sb_k', 128),
            ('sb_v', 128), ('gmlp', 1024), ('gate', N_BRANCH * D_MODEL))


def _cp(n_axes, vmem_mb=None):
    return pltpu.CompilerParams(
        dimension_semantics=("arbitrary",) * n_axes,
        vmem_limit_bytes=None if vmem_mb is None else vmem_mb * 2 ** 20)


def _dot(a, b):
    return jnp.dot(a, b, preferred_element_type=F32)


def _dot_nt(a, b):
    return lax.dot_general(a, b, (((1,), (1,)), ((), ())), preferred_element_type=F32)


def _rms_rows(x, g):
    return x * lax.rsqrt(jnp.mean(x * x, axis=-1, keepdims=True) + EPS) * g


def _ln_rows(x, g, b):
    xc = x - jnp.mean(x, axis=-1, keepdims=True)
    return xc * lax.rsqrt(jnp.mean(xc * xc, axis=-1, keepdims=True) + EPS) * g + b


def _logsig(z):
    return jnp.minimum(z, 0.0) - jnp.log1p(jnp.exp(-jnp.abs(z)))


def _split_bf16(x):
    hi = x.astype(BF16)
    lo = (x - hi.astype(F32)).astype(BF16)
    return hi, lo


def _rms_matmul_body(x_ref, g_ref, w_ref, z_ref, xn_ref):
    @pl.when(pl.program_id(1) == 0)
    def _():
        xn_ref[...] = _rms_rows(x_ref[...], g_ref[...]).astype(BF16)

    z_ref[...] = _dot(xn_ref[...], w_ref[...])


def rms_matmul(x, g, w_all, layer, tm, tn):
    m, k = x.shape
    n = w_all.shape[-1]
    return pl.pallas_call(
        _rms_matmul_body,
        grid=(m // tm, n // tn),
        in_specs=[pl.BlockSpec((tm, k), lambda i, j: (i, 0)),
                  pl.BlockSpec((None, 1, k), lambda i, j: (layer, 0, 0)),
                  pl.BlockSpec((None, k, tn), lambda i, j: (layer, 0, j))],
        out_specs=[pl.BlockSpec((tm, tn), lambda i, j: (i, j)),
                   pl.BlockSpec((tm, k), lambda i, j: (i, 0))],
        out_shape=[jax.ShapeDtypeStruct((m, n), F32), jax.ShapeDtypeStruct((m, k), BF16)],
        compiler_params=_cp(2, 48),
        name="rms_matmul",
    )(x, g, w_all)


def _rope_slab(x, c, sa, sb):
    return x * c + pltpu.roll(x, 16, 1) * sa + pltpu.roll(x, LANES - 16, 1) * sb


def _prep_body(z_ref, c_ref, sa_ref, sb_ref, gqn_ref, gkvn_ref, wuq_ref, wk_ref, gq_ref, gk_ref, place_ref,
               glg_ref, glb_ref, gw_ref, gb_ref,
               a_ref, q_ref, k_ref, ckv_ref, kpe_ref, sbq_ref, outd_ref, vln_ref, *, is_prompt, tm):
    c, sa, sb = c_ref[...], sa_ref[...], sb_ref[...]
    a_ref[...] = z_ref[:, Z_CONV:Z_CONV + 512] * jax.nn.sigmoid(z_ref[:, Z_CONV + 512:Z_CONV + 1024])
    cq = _rms_rows(z_ref[:, Z_MQ:Z_MQ + 512], gqn_ref[...]).astype(BF16)
    ckv = _rms_rows(z_ref[:, Z_MKV:Z_MKV + 256], gkvn_ref[...])
    ckv_ref[...] = ckv
    kpe = _rope_slab(z_ref[:, Z_KR:Z_KR + LANES], c, sa, sb)
    kpe_ref[...] = kpe
    q0 = _dot(cq, wuq_ref[...])
    kcat = jnp.concatenate([ckv, kpe], axis=-1).astype(BF16)
    k0 = _dot(kcat, wk_ref[...])
    gq, gk = gq_ref[...], gk_ref[...]
    for h in range(MLA_HEADS):
        sl = slice(h * HEAD_PAD, (h + 1) * HEAD_PAD)
        qh = _rope_slab(q0[:, sl], c, sa, sb)
        qh = qh * lax.rsqrt(jnp.sum(qh * qh, -1, keepdims=True) * (1.0 / MLA_QK) + EPS) * gq
        q_ref[:, sl] = qh.astype(BF16)
        kh = k0[:, sl]
        kh = kh * lax.rsqrt(jnp.sum(kh * kh, -1, keepdims=True) * (1.0 / MLA_QK) + EPS) * gk
        k_ref[:, sl] = kh.astype(BF16)
    sq = (z_ref[:, Z_SQ:Z_SQ + 512] * SB_SCALE).astype(BF16)
    sbq_ref[...] = _dot(sq, place_ref[...]).astype(BF16)
    ge = jax.nn.gelu(z_ref[:, Z_GMLP:Z_GMLP + 1024])
    u = ge[:, :512]
    vln = _ln_rows(ge[:, 512:], glg_ref[...], glb_ref[...])
    vln_ref[...] = vln
    if is_prompt:
        vb = vln.astype(BF16)
        for ch in range(tm // CHUNK):
            rs = slice(ch * CHUNK, (ch + 1) * CHUNK)
            for g in range(GMLP_GROUPS):
                cs = slice(g * 128, (g + 1) * 128)
                mix = _dot(gw_ref[g], vb[rs, cs]) + gb_ref[g]
                outd_ref[rs, cs] = (u[rs, cs] * mix).astype(BF16)
    else:
        outd_ref[...] = (u * (vln * gw_ref[...] + gb_ref[...])).astype(BF16)


def branch_prep(z, tabs, wts, layer, is_prompt, tm):
    m = z.shape[0]
    c_t, sa_t, sb_t = tabs
    row = lambda i: (i, 0)
    lay3 = lambda i: (layer, 0, 0)
    if is_prompt:
        gw, gb = wts['g_wtril'], wts['g_bias']
        gw_spec = pl.BlockSpec((None, GMLP_GROUPS, CHUNK, CHUNK), lambda i: (layer, 0, 0, 0))
        gb_spec = pl.BlockSpec((None, GMLP_GROUPS, CHUNK, CHUNK), lambda i: (layer, 0, 0, 0))
    else:
        gw, gb = wts['g_wdiag'], wts['g_brow']
        gw_spec = pl.BlockSpec((None, 1, 512), lay3)
        gb_spec = pl.BlockSpec((None, 1, 512), lay3)
    in_specs = [
        pl.BlockSpec((tm, Z_W), row),
        pl.BlockSpec((tm, LANES), row), pl.BlockSpec((tm, LANES), row), pl.BlockSpec((tm, LANES), row),
        pl.BlockSpec((None, 1, 512), lay3), pl.BlockSpec((None, 1, 256), lay3),
        pl.BlockSpec((None, 512, 1024), lay3), pl.BlockSpec((None, 384, 1024), lay3),
        pl.BlockSpec((None, 1, LANES), lay3), pl.BlockSpec((None, 1, LANES), lay3),
        pl.BlockSpec((512, 1024), lambda i: (0, 0)),
        pl.BlockSpec((None, 1, 512), lay3), pl.BlockSpec((None, 1, 512), lay3),
        gw_spec, gb_spec,
    ]
    outs = [(512, F32), (1024, BF16), (1024, BF16), (256, F32), (LANES, F32), (1024, BF16), (512, BF16), (512, F32)]
    return pl.pallas_call(
        functools.partial(_prep_body, is_prompt=is_prompt, tm=tm),
        grid=(m // tm,),
        in_specs=in_specs,
        out_specs=[pl.BlockSpec((tm, w), row) for w, _ in outs],
        out_shape=[jax.ShapeDtypeStruct((m, w), dt) for w, dt in outs],
        compiler_params=_cp(1, 48),
        name="branch_prep_p" if is_prompt else "branch_prep_s",
    )(z, c_t, sa_t, sb_t, wts['g_qn'], wts['g_kvn'], wts['wuq'], wts['wk'], wts['gq'], wts['gk'], wts['place_sb'],
      wts['g_ln_g'], wts['g_ln_b'], gw, gb)


CONV_HALO = 32


def _conv_p_body(a_ref, w_ref, b_ref, g_ref, bb_ref, o_ref, buf_ref, y_ref, *, tl):
    @pl.when(pl.program_id(1) == 0)
    def _():
        buf_ref[0:CONV_HALO, :] = jnp.zeros((CONV_HALO, 512), F32)

    buf_ref[CONV_HALO:CONV_HALO + tl, :] = a_ref[...]
    rb = 128
    shift = CONV_HALO - (CONV_WIDTH - 1)
    for r in range(tl // rb):
        for cb in range(4):
            cs = slice(cb * LANES, (cb + 1) * LANES)
            acc = jnp.zeros((rb, LANES), F32)
            for j in range(CONV_WIDTH):
                lo = r * rb + j + shift
                acc = acc + w_ref[j:j + 1, cs] * buf_ref[lo:lo + rb, cs]
            y_ref[r * rb:(r + 1) * rb, cs] = acc
    y = _ln_rows(y_ref[...] + b_ref[...], g_ref[...], bb_ref[...])
    o_ref[...] = jax.nn.silu(y).astype(BF16)
    buf_ref[0:CONV_HALO, :] = buf_ref[tl:tl + CONV_HALO, :]


def conv_prompt(a, wts, layer, n_batch, seq, tl=256):
    nt = seq // tl
    lay3 = lambda b, i: (layer, 0, 0)
    return pl.pallas_call(
        functools.partial(_conv_p_body, tl=tl),
        grid=(n_batch, nt),
        in_specs=[pl.BlockSpec((tl, 512), lambda b, i: (b * nt + i, 0)),
                  pl.BlockSpec((None, CONV_WIDTH, 512), lay3),
                  pl.BlockSpec((None, 1, 512), lay3), pl.BlockSpec((None, 1, 512), lay3),
                  pl.BlockSpec((None, 1, 512), lay3)],
        out_specs=pl.BlockSpec((tl, 512), lambda b, i: (b * nt + i, 0)),
        out_shape=jax.ShapeDtypeStruct((n_batch * seq, 512), BF16),
        scratch_shapes=[pltpu.VMEM((tl + CONV_HALO, 512), F32), pltpu.VMEM((tl, 512), F32)],
        compiler_params=_cp(2),
        name="conv_prompt",
    )(a, wts['conv_w'], wts['conv_b'], wts['conv_ln_g'], wts['conv_ln_b'])


def _conv_s_body(st_ref, a_ref, w_ref, b_ref, g_ref, bb_ref, o_ref):
    w = w_ref[...]
    y = jnp.sum(st_ref[...] * w[None, :CONV_WIDTH - 1, :], axis=1)
    y = y + a_ref[...] * w[CONV_WIDTH - 1:CONV_WIDTH, :] + b_ref[...]
    o_ref[...] = jax.nn.silu(_ln_rows(y, g_ref[...], bb_ref[...])).astype(BF16)


def conv_sample(state_all, a, wts, layer):
    nb = a.shape[0]
    lay3 = lambda i: (layer, 0, 0)
    return pl.pallas_call(
        _conv_s_body,
        grid=(1,),
        in_specs=[pl.BlockSpec((None, nb, CONV_WIDTH - 1, 512), lambda i: (layer, 0, 0, 0)),
                  pl.BlockSpec((nb, 512), lambda i: (0, 0)),
                  pl.BlockSpec((None, CONV_WIDTH, 512), lay3),
                  pl.BlockSpec((None, 1, 512), lay3), pl.BlockSpec((None, 1, 512), lay3),
                  pl.BlockSpec((None, 1, 512), lay3)],
        out_specs=pl.BlockSpec((nb, 512), lambda i: (0, 0)),
        out_shape=jax.ShapeDtypeStruct((nb, 512), BF16),
        compiler_params=_cp(1, 48),
        name="conv_sample",
    )(state_all, a, wts['conv_w'], wts['conv_b'], wts['conv_ln_g'], wts['conv_ln_b'])


def _pair_tables(n_tiles, descending):
    qs, ks = [], []
    for qi in range(n_tiles):
        kis = range(qi, -1, -1) if descending else range(qi + 1)
        for ki in kis:
            qs.append(qi)
            ks.append(ki)
    return np.asarray(qs, np.int32), np.asarray(ks, np.int32)


def _mla_flash_body(qt_ref, kt_ref, q_ref, k_ref, c_ref, wbd_ref, o_ref, m_ref, l_ref, acc_ref, *, tq, tk):
    s_idx = pl.program_id(1)
    qi, ki = qt_ref[s_idx], kt_ref[s_idx]

    @pl.when(ki == 0)
    def _():
        m_ref[...] = jnp.full(m_ref.shape, NEG_BIG, F32)
        l_ref[...] = jnp.zeros(l_ref.shape, F32)
        acc_ref[...] = jnp.zeros(acc_ref.shape, F32)

    def step(masked):
        c = c_ref[...].astype(BF16)
        if masked:
            allow = (lax.broadcasted_iota(jnp.int32, (tq, tk), 1) <= lax.broadcasted_iota(jnp.int32, (tq, tk), 0))
        for h in range(MLA_HEADS):
            sl = slice(h * HEAD_PAD, (h + 1) * HEAD_PAD)
            s = _dot_nt(q_ref[:, sl], k_ref[:, sl])
            if masked:
                s = jnp.where(allow, s, NEG_BIG)
            m_prev = m_ref[h]
            m_new = jnp.maximum(m_prev, jnp.max(s, -1, keepdims=True))
            alpha = jnp.exp(m_prev - m_new)
            p = jnp.exp(s - m_new[:, :1])
            l_ref[h] = alpha * l_ref[h] + jnp.sum(p, -1, keepdims=True)
            acc_ref[h] = alpha[:, :1] * acc_ref[h] + _dot(p.astype(BF16), c)
            m_ref[h] = m_new

    @pl.when(ki < qi)
    def _():
        step(False)

    @pl.when(ki == qi)
    def _():
        step(True)
        o = jnp.concatenate(
            [(acc_ref[h] * (1.0 / l_ref[h][:, :1])).astype(BF16) for h in range(MLA_HEADS)], axis=-1)
        o_ref[...] = _dot(o, wbd_ref[...]).astype(BF16)


def mla_prompt(q, k, ckv, wts, layer, n_batch, seq, tq=512):
    nt = seq // tq
    qs, ks = _pair_tables(nt, descending=False)
    qmap = lambda b, s, qt, kt: (b * nt + qt[s], 0)
    kmap = lambda b, s, qt, kt: (b * nt + kt[s], 0)
    grid_spec = pltpu.PrefetchScalarGridSpec(
        num_scalar_prefetch=2,
        grid=(n_batch, len(qs)),
        in_specs=[pl.BlockSpec((tq, 1024), qmap), pl.BlockSpec((tq, 1024), kmap),
                  pl.BlockSpec((tq, MLA_KV_LORA), kmap),
                  pl.BlockSpec((None, MLA_HEADS * MLA_KV_LORA, 512), lambda b, s, qt, kt: (layer, 0, 0))],
        out_specs=pl.BlockSpec((tq, 512), qmap),
        scratch_shapes=[pltpu.VMEM((MLA_HEADS, tq, LANES), F32), pltpu.VMEM((MLA_HEADS, tq, LANES), F32),
                        pltpu.VMEM((MLA_HEADS, tq, MLA_KV_LORA), F32)],
    )
    return pl.pallas_call(
        functools.partial(_mla_flash_body, tq=tq, tk=tq),
        grid_spec=grid_spec,
        out_shape=jax.ShapeDtypeStruct((n_batch * seq, 512), BF16),
        compiler_params=_cp(2, 48),
        name="mla_prompt",
    )(jnp.asarray(qs), jnp.asarray(ks), q, k, ckv, wts['wbd'])


def _sb_tile(z, carry, u_mat, v_bf, allow):
    lb = _logsig(z)
    lk = lb - z
    if allow is not None:
        lk = jnp.where(allow, lk, 0.0)
    hi, lo = _split_bf16(lk)
    later = carry + _dot(hi, u_mat) + _dot(lo, u_mat)
    a = jnp.exp(lb + later)
    if allow is not None:
        a = jnp.where(allow, a, 0.0)
    return _dot(a.astype(BF16), v_bf), carry + jnp.sum(lk, -1, keepdims=True)


def _pick_group_half(acc_lo, acc_hi, group):
    lane = lax.broadcasted_iota(jnp.int32, acc_lo.shape, 1)
    if group == 0:
        return jnp.where(lane < 64, acc_lo, pltpu.roll(acc_hi, 64, 1))
    return jnp.where(lane < 64, pltpu.roll(acc_lo, 64, 1), acc_hi)


def _sb_flash_body(qt_ref, kt_ref, q_ref, k_ref, v_ref, u_ref, o_ref, carry_ref, acc_ref, alive_ref, *, tq, tk):
    s_idx = pl.program_id(1)
    qi, ki = qt_ref[s_idx], kt_ref[s_idx]

    @pl.when(ki == qi)
    def _():
        carry_ref[...] = jnp.zeros(carry_ref.shape, F32)
        acc_ref[...] = jnp.zeros(acc_ref.shape, F32)
        alive_ref[0] = 1

    def step(masked):
        k_bf = k_ref[...].astype(BF16)
        v_bf = v_ref[...].astype(BF16)
        u_mat = u_ref[...]
        allow = None
        if masked:
            allow = (lax.broadcasted_iota(jnp.int32, (tq, tk), 1) < lax.broadcasted_iota(jnp.int32, (tq, tk), 0))
        worst = None
        for h in range(SB_HEADS):
            z = _dot_nt(q_ref[:, h * LANES:(h + 1) * LANES], k_bf)
            pv, carry = _sb_tile(z, carry_ref[h][:, :1], u_mat, v_bf, allow)
            acc_ref[h] = acc_ref[h] + pv
            carry_ref[h] = jnp.broadcast_to(carry, (tq, LANES))
            hmax = jnp.max(carry)
            worst = hmax if worst is None else jnp.maximum(worst, hmax)
        alive_ref[0] = (worst >= SB_DEAD).astype(jnp.int32)

    @pl.when(jnp.logical_and(ki == qi, alive_ref[0] == 1))
    def _():
        step(True)

    @pl.when(jnp.logical_and(ki < qi, alive_ref[0] == 1))
    def _():
        step(False)

    @pl.when(ki == 0)
    def _():
        for m in range(SB_HEADS // 2):
            blk = _pick_group_half(acc_ref[2 * m], acc_ref[2 * m + 1], (2 * m) // SB_REP)
            o_ref[:, m * LANES:(m + 1) * LANES] = blk.astype(BF16)


def sb_prompt(sbq, z, wts, n_batch, seq, tq=512):
    nt = seq // tq
    qs, ks = _pair_tables(nt, descending=True)
    kcol, vcol = Z_SK // LANES, Z_SV // LANES
    grid_spec = pltpu.PrefetchScalarGridSpec(
        num_scalar_prefetch=2,
        grid=(n_batch, len(qs)),
        in_specs=[pl.BlockSpec((tq, 1024), lambda b, s, qt, kt: (b * nt + qt[s], 0)),
                  pl.BlockSpec((tq, LANES), lambda b, s, qt, kt: (b * nt + kt[s], kcol)),
                  pl.BlockSpec((tq, LANES), lambda b, s, qt, kt: (b * nt + kt[s], vcol)),
                  pl.BlockSpec((tq, tq), lambda b, s, qt, kt: (0, 0))],
        out_specs=pl.BlockSpec((tq, 512), lambda b, s, qt, kt: (b * nt + qt[s], 0)),
        scratch_shapes=[pltpu.VMEM((SB_HEADS, tq, LANES), F32), pltpu.VMEM((SB_HEADS, tq, LANES), F32),
                        pltpu.SMEM((1,), jnp.int32)],
    )
    return pl.pallas_call(
        functools.partial(_sb_flash_body, tq=tq, tk=tq),
        grid_spec=grid_spec,
        out_shape=jax.ShapeDtypeStruct((n_batch * seq, 512), BF16),
        compiler_params=_cp(2, 48),
        name="sb_prompt",
    )(jnp.asarray(qs), jnp.asarray(ks), sbq, z, z, wts['u512'])


def _mix_body(xn_ref, a_ref, b_ref, c_ref, d_ref, wg_ref, wb_ref, o_ref):
    xn = xn_ref[...]
    acc = None
    for i, br in enumerate((a_ref, b_ref, c_ref, d_ref)):
        term = jax.nn.sigmoid(_dot(xn, wg_ref[i])) * _dot(br[...], wb_ref[i])
        acc = term if acc is None else acc + term
    o_ref[...] = acc.astype(BF16)


def gated_mix(xn, branches, wts, layer, tm, tn=256):
    m = xn.shape[0]
    row = lambda i, j: (i, 0)
    return pl.pallas_call(
        _mix_body,
        grid=(m // tm, D_MODEL // tn),
        in_specs=[pl.BlockSpec((tm, D_MODEL), row)] + [pl.BlockSpec((tm, MIX_W), row)] * 4 + [
            pl.BlockSpec((None, N_BRANCH, D_MODEL, tn), lambda i, j: (layer, 0, 0, j)),
            pl.BlockSpec((None, N_BRANCH, MIX_W, tn), lambda i, j: (layer, 0, 0, j))],
        out_specs=pl.BlockSpec((tm, tn), lambda i, j: (i, j)),
        out_shape=jax.ShapeDtypeStruct((m, D_MODEL), BF16),
        compiler_params=_cp(2, 48),
        name="gated_mix",
    )(xn, *branches, wts['w_gate'], wts['w_branch'])


def _route(logits):
    lane = lax.broadcasted_iota(jnp.int32, logits.shape, 1)
    lane_f = lane.astype(F32)
    far = float(LANES)
    gmask = (lane >> 2) == (N_EXPERTS // 4)
    gl = jnp.where(gmask, logits, NEG_BIG)
    gmax = jnp.max(gl, -1, keepdims=True)
    g_p = 1.0 / jnp.sum(jnp.where(gmask, jnp.exp(gl - gmax), 0.0), -1, keepdims=True)
    gidx = jnp.min(jnp.where(gl == gmax, lane_f, far), -1, keepdims=True) - float(N_EXPERTS)
    el = jnp.where((lane >> 3).astype(F32) == gidx, logits, NEG_BIG)
    m1 = jnp.max(el, -1, keepdims=True)
    i1 = jnp.min(jnp.where(el == m1, lane_f, far), -1, keepdims=True)
    el2 = jnp.where(lane_f == i1, NEG_BIG, el)
    m2 = jnp.max(el2, -1, keepdims=True)
    i2 = jnp.min(jnp.where(el2 == m2, lane_f, far), -1, keepdims=True)
    e2 = jnp.exp(m2 - m1)
    inv = 1.0 / (1.0 + e2)
    return jnp.where(lane_f == i1, inv * g_p, 0.0) + jnp.where(lane_f == i2, e2 * inv * g_p, 0.0)


def _out_router_body(r_ref, mix_ref, wo_ref, g_ref, whi_ref, wlo_ref, rb_ref, r1_ref, hn_ref, comb_ref):
    r1 = r_ref[...] + _dot(mix_ref[...], wo_ref[...])
    r1_ref[...] = r1
    h = _rms_rows(r1, g_ref[...])
    hi, lo = _split_bf16(h)
    hn_ref[...] = hi
    logits = _dot(hi, whi_ref[...]) + _dot(lo, whi_ref[...]) + _dot(hi, wlo_ref[...]) + rb_ref[...]
    comb_ref[...] = _route(logits)


def out_router(r, mixed, wts, layer, tm):
    m = r.shape[0]
    row = lambda i: (i, 0)
    lay3 = lambda i: (layer, 0, 0)
    return pl.pallas_call(
        _out_router_body,
        grid=(m // tm,),
        in_specs=[pl.BlockSpec((tm, D_MODEL), row), pl.BlockSpec((tm, D_MODEL), row),
                  pl.BlockSpec((None, D_MODEL, D_MODEL), lay3), pl.BlockSpec((None, 1, D_MODEL), lay3),
                  pl.BlockSpec((None, D_MODEL, LANES), lay3), pl.BlockSpec((None, D_MODEL, LANES), lay3),
                  pl.BlockSpec((None, 1, LANES), lay3)],
        out_specs=[pl.BlockSpec((tm, D_MODEL), row), pl.BlockSpec((tm, D_MODEL), row),
                   pl.BlockSpec((tm, LANES), row)],
        out_shape=[jax.ShapeDtypeStruct((m, D_MODEL), F32), jax.ShapeDtypeStruct((m, D_MODEL), BF16),
                   jax.ShapeDtypeStruct((m, LANES), F32)],
        compiler_params=_cp(1, 48),
        name="out_router",
    )(r, mixed, wts['w_out'], wts['g_ffn'], wts['wr_hi'], wts['wr_lo'], wts['r_bias'])


def _moe_body(hn_ref, comb_ref, wgu_ref, wd_ref, o_ref):
    e = pl.program_id(1)

    @pl.when(e == 0)
    def _():
        o_ref[...] = jnp.zeros(o_ref.shape, F32)

    gu = _dot(hn_ref[...], wgu_ref[...])
    comb = comb_ref[...]
    lane = lax.broadcasted_iota(jnp.int32, comb.shape, 1)
    ce = jnp.sum(jnp.where(lane == e, comb, 0.0), -1, keepdims=True)
    hid = jax.nn.silu(gu[:, :D_EXPERT]) * gu[:, D_EXPERT:] * ce
    o_ref[...] += _dot(hid.astype(BF16), wd_ref[...])


def moe_dense(hn, comb, wts, layer, tm):
    m = hn.shape[0]
    row = lambda i, e: (i, 0)
    return pl.pallas_call(
        _moe_body,
        grid=(m // tm, N_EXPERTS),
        in_specs=[pl.BlockSpec((tm, D_MODEL), row), pl.BlockSpec((tm, LANES), row),
                  pl.BlockSpec((None, None, D_MODEL, 2 * D_EXPERT), lambda i, e: (layer, e, 0, 0)),
                  pl.BlockSpec((None, None, D_EXPERT, D_MODEL), lambda i, e: (layer, e, 0, 0))],
        out_specs=pl.BlockSpec((tm, D_MODEL), row),
        out_shape=jax.ShapeDtypeStruct((m, D_MODEL), F32),
        compiler_params=_cp(2, 48),
        name="moe_dense",
    )(hn, comb, wts['wgu'], wts['wd'])


def _ple_body(r_ref, f_ref, rt_ref, ft_ref, p_ref, g_ref, wg_ref, wp_ref, o_ref, xn_ref):
    @pl.when(pl.program_id(1) == 0)
    def _():
        xn_ref[...] = _rms_rows(r_ref[...] + f_ref[...], g_ref[...]).astype(BF16)

    gate = jax.nn.sigmoid(_dot(xn_ref[...], wg_ref[...]))
    o_ref[...] = rt_ref[...] + ft_ref[...] + gate * _dot(p_ref[...].astype(BF16), wp_ref[...])


def ple(r, ffn, p_all, wts, layer, tm, tn=512):
    m = r.shape[0]
    return pl.pallas_call(
        _ple_body,
        grid=(m // tm, D_MODEL // tn),
        in_specs=[pl.BlockSpec((tm, D_MODEL), lambda i, j: (i, 0)), pl.BlockSpec((tm, D_MODEL), lambda i, j: (i, 0)),
                  pl.BlockSpec((tm, tn), lambda i, j: (i, j)), pl.BlockSpec((tm, tn), lambda i, j: (i, j)),
                  pl.BlockSpec((None, tm, PLE_DIM), lambda i, j: (layer, i, 0)),
                  pl.BlockSpec((None, 1, D_MODEL), lambda i, j: (layer, 0, 0)),
                  pl.BlockSpec((None, D_MODEL, tn), lambda i, j: (layer, 0, j)),
                  pl.BlockSpec((None, PLE_DIM, tn), lambda i, j: (layer, 0, j))],
        out_specs=pl.BlockSpec((tm, tn), lambda i, j: (i, j)),
        out_shape=jax.ShapeDtypeStruct((m, D_MODEL), F32),
        scratch_shapes=[pltpu.VMEM((tm, D_MODEL), BF16)],
        compiler_params=_cp(2, 48),
        name="ple",
    )(r, ffn, r, ffn, p_all, wts['g_ple'], wts['w_ple_gate'], wts['w_ple_proj'])


def _mla_sprep_body(q_ref, gk_ref, wukt_ref, sel_ref, qabs_ref, qrope_ref):
    gk = gk_ref[...]
    for h in range(MLA_HEADS):
        sl = slice(h * HEAD_PAD, (h + 1) * HEAD_PAD)
        qg = (q_ref[:, sl].astype(F32) * gk).astype(BF16)
        qabs_ref[:, h * MLA_KV_LORA:(h + 1) * MLA_KV_LORA] = _dot(qg, wukt_ref[h]).astype(BF16)
        qrope_ref[:, sl] = _dot(qg, sel_ref[...]).astype(BF16)


def mla_sample_prep(q, wts, layer):
    nb = q.shape[0]
    return pl.pallas_call(
        _mla_sprep_body,
        grid=(1,),
        in_specs=[pl.BlockSpec((nb, 1024), lambda i: (0, 0)),
                  pl.BlockSpec((None, 1, LANES), lambda i: (layer, 0, 0)),
                  pl.BlockSpec((None, MLA_HEADS, HEAD_PAD, MLA_KV_LORA), lambda i: (layer, 0, 0, 0)),
                  pl.BlockSpec((LANES, LANES), lambda i: (0, 0))],
        out_specs=[pl.BlockSpec((nb, MLA_HEADS * MLA_KV_LORA), lambda i: (0, 0)),
                   pl.BlockSpec((nb, 1024), lambda i: (0, 0))],
        out_shape=[jax.ShapeDtypeStruct((nb, MLA_HEADS * MLA_KV_LORA), BF16),
                   jax.ShapeDtypeStruct((nb, 1024), BF16)],
        compiler_params=_cp(1),
        name="mla_sample_prep",
    )(q, wts['gk'], wts['wukt_pad'], wts['sel_rope'])


MLA_PPS = 16
MLA_SUB = 4


def _mla_dec_body(pt_ref, q8_ref, k8_ref, qabs_ref, qrope_ref, cnew_ref, wukt_ref, *rest, n_chunks):
    ckv_refs = rest[:MLA_PPS]
    kpe_refs = rest[MLA_PPS:2 * MLA_PPS]
    o_ref, m_ref, l_ref, acc_ref = rest[2 * MLA_PPS:]
    ci = pl.program_id(1)

    @pl.when(ci == 0)
    def _():
        m_ref[...] = jnp.full(m_ref.shape, NEG_BIG, F32)
        l_ref[...] = jnp.zeros(l_ref.shape, F32)
        acc_ref[...] = jnp.zeros(acc_ref.shape, F32)

    qabs = qabs_ref[...]
    qrope = qrope_ref[:, :MLA_ROPE]
    ones = jnp.ones((MLA_HEADS, MLA_ROPE), BF16)
    wukt = wukt_ref[...]
    for sub in range(MLA_PPS // MLA_SUB):
        pages = range(sub * MLA_SUB, (sub + 1) * MLA_SUB)
        c = jnp.concatenate([ckv_refs[i][...] for i in pages], axis=0).astype(BF16)
        kp = jnp.concatenate([kpe_refs[i][...] for i in pages], axis=0)
        tk = c.shape[0]
        kt = _dot_nt(wukt, c)
        ssn = jnp.sum((kt * kt).reshape(MLA_HEADS, MLA_NOPE, tk), axis=1)
        ssr = _dot_nt(ones, (kp * kp).astype(BF16))
        raw = _dot_nt(qabs, c) + _dot_nt(qrope, kp.astype(BF16))
        s = raw * lax.rsqrt((ssn + ssr) * (1.0 / MLA_QK) + EPS)
        m_prev = m_ref[...]
        m_new = jnp.maximum(m_prev, jnp.max(s, -1, keepdims=True))
        alpha = jnp.exp(m_prev - m_new)
        p = jnp.exp(s - m_new[:, :1])
        l_ref[...] = alpha * l_ref[...] + jnp.sum(p, -1, keepdims=True)
        acc_ref[...] = alpha[:, :1] * acc_ref[...] + _dot(p.astype(BF16), c)
        m_ref[...] = m_new

    @pl.when(ci == n_chunks - 1)
    def _():
        s_self = jnp.sum(q8_ref[...].astype(F32) * k8_ref[...].astype(F32), -1, keepdims=True)
        m_prev = m_ref[...]
        m_fin = jnp.maximum(m_prev, s_self)
        alpha = jnp.exp(m_prev - m_fin)
        p_self = jnp.exp(s_self - m_fin[:, :1])
        l_fin = alpha[:, :1] * l_ref[:, :1] + p_self
        acc = alpha[:, :1] * acc_ref[...] + p_self * cnew_ref[...]
        o_ref[...] = acc * (1.0 / l_fin)


def mla_sample(q, k, qabs, qrope, ckv_new, cache_ckv, cache_kpe, page_flat, wts, layer, n_pages):
    nb = q.shape[0]
    n_chunks = n_pages // MLA_PPS
    page = cache_ckv.shape[2]
    seq3 = lambda b, c, pt: (b, 0, 0)

    def cache_spec(width, i):
        return pl.BlockSpec((None, None, page, width),
                            lambda b, c, pt: (layer, pt[b * n_pages + c * MLA_PPS + i], 0, 0))

    grid_spec = pltpu.PrefetchScalarGridSpec(
        num_scalar_prefetch=1,
        grid=(nb, n_chunks),
        in_specs=[pl.BlockSpec((None, MLA_HEADS, HEAD_PAD), seq3), pl.BlockSpec((None, MLA_HEADS, HEAD_PAD), seq3),
                  pl.BlockSpec((None, MLA_HEADS, MLA_KV_LORA), seq3), pl.BlockSpec((None, MLA_HEADS, HEAD_PAD), seq3),
                  pl.BlockSpec((None, 1, MLA_KV_LORA), seq3),
                  pl.BlockSpec((None, MLA_HEADS * MLA_NOPE, MLA_KV_LORA), lambda b, c, pt: (layer, 0, 0))]
        + [cache_spec(MLA_KV_LORA, i) for i in range(MLA_PPS)]
        + [cache_spec(MLA_ROPE, i) for i in range(MLA_PPS)],
        out_specs=pl.BlockSpec((None, MLA_HEADS, MLA_KV_LORA), seq3),
        scratch_shapes=[pltpu.VMEM((MLA_HEADS, LANES), F32), pltpu.VMEM((MLA_HEADS, LANES), F32),
                        pltpu.VMEM((MLA_HEADS, MLA_KV_LORA), F32)],
    )
    return pl.pallas_call(
        functools.partial(_mla_dec_body, n_chunks=n_chunks),
        grid_spec=grid_spec,
        out_shape=jax.ShapeDtypeStruct((nb, MLA_HEADS, MLA_KV_LORA), F32),
        compiler_params=_cp(2, 48),
        name="mla_sample",
    )(page_flat, q.reshape(nb, MLA_HEADS, HEAD_PAD), k.reshape(nb, MLA_HEADS, HEAD_PAD),
      qabs.reshape(nb, MLA_HEADS, MLA_KV_LORA), qrope.reshape(nb, MLA_HEADS, HEAD_PAD),
      ckv_new.reshape(nb, 1, MLA_KV_LORA), wts['wukt'], *([cache_ckv] * MLA_PPS), *([cache_kpe] * MLA_PPS))


def _latent_value_body(o_ref, wbd_ref, out_ref):
    out_ref[...] = _dot(o_ref[...].astype(BF16), wbd_ref[...]).astype(BF16)


def latent_value(o_lat, wts, layer):
    nb = o_lat.shape[0]
    return pl.pallas_call(
        _latent_value_body,
        grid=(1,),
        in_specs=[pl.BlockSpec((nb, MLA_HEADS * MLA_KV_LORA), lambda i: (0, 0)),
                  pl.BlockSpec((None, MLA_HEADS * MLA_KV_LORA, 512), lambda i: (layer, 0, 0))],
        out_specs=pl.BlockSpec((nb, 512), lambda i: (0, 0)),
        out_shape=jax.ShapeDtypeStruct((nb, 512), BF16),
        compiler_params=_cp(1),
        name="latent_value",
    )(o_lat, wts['wbd'])


SB_PPS = 16
SB_SUB = 2


def _sb_dec_body(pt_ref, q_ref, u_ref, *rest):
    k_refs = rest[:SB_PPS]
    v_refs = rest[SB_PPS:2 * SB_PPS]
    o_ref, carry_ref, acc_ref, alive_ref = rest[2 * SB_PPS:]
    ci = pl.program_id(1)

    @pl.when(ci == 0)
    def _():
        carry_ref[...] = jnp.zeros(carry_ref.shape, F32)
        acc_ref[...] = jnp.zeros(acc_ref.shape, F32)
        alive_ref[0] = 1

    @pl.when(alive_ref[0] == 1)
    def _():
        q = q_ref[...]
        u_mat = u_ref[...]
        carry = carry_ref[:, :1]
        acc = acc_ref[...]
        for sub in range(SB_PPS // SB_SUB - 1, -1, -1):
            pages = range(sub * SB_SUB, (sub + 1) * SB_SUB)
            k_bf = jnp.concatenate([k_refs[i][...] for i in pages], axis=0).astype(BF16)
            v_bf = jnp.concatenate([v_refs[i][...] for i in pages], axis=0).astype(BF16)
            pv, carry = _sb_tile(_dot_nt(q, k_bf), carry, u_mat, v_bf, None)
            acc = acc + pv
        acc_ref[...] = acc
        carry_ref[...] = jnp.broadcast_to(carry, carry_ref.shape)
        alive_ref[0] = (jnp.max(carry) >= SB_DEAD).astype(jnp.int32)

    @pl.when(ci == pl.num_programs(1) - 1)
    def _():
        acc = acc_ref[...]
        row = lax.broadcasted_iota(jnp.int32, acc.shape, 0)
        o_ref[...] = jnp.where(row < SB_REP, acc, pltpu.roll(acc, 64, 1))[:, :SB_HEAD_DIM]


def sb_sample(sbq, cache_k, cache_v, page_flat, wts, layer, n_pages):
    nb = sbq.shape[0]
    n_chunks = n_pages // SB_PPS
    page = cache_k.shape[2]

    def cache_spec(i):
        return pl.BlockSpec((None, None, page, LANES),
                            lambda b, c, pt: (layer, pt[b * n_pages + (n_chunks - 1 - c) * SB_PPS + i], 0, 0))

    grid_spec = pltpu.PrefetchScalarGridSpec(
        num_scalar_prefetch=1,
        grid=(nb, n_chunks),
        in_specs=[pl.BlockSpec((None, SB_HEADS, LANES), lambda b, c, pt: (b, 0, 0)),
                  pl.BlockSpec((SB_SUB * page, SB_SUB * page), lambda b, c, pt: (0, 0))]
        + [cache_spec(i) for i in range(SB_PPS)] * 2,
        out_specs=pl.BlockSpec((None, SB_HEADS, SB_HEAD_DIM), lambda b, c, pt: (b, 0, 0)),
        scratch_shapes=[pltpu.VMEM((SB_HEADS, LANES), F32), pltpu.VMEM((SB_HEADS, LANES), F32),
                        pltpu.SMEM((1,), jnp.int32)],
    )
    return pl.pallas_call(
        _sb_dec_body,
        grid_spec=grid_spec,
        out_shape=jax.ShapeDtypeStruct((nb, SB_HEADS, SB_HEAD_DIM), F32),
        compiler_params=_cp(2, 48),
        name="sb_sample",
    )(page_flat, sbq.reshape(nb, SB_HEADS, LANES), wts['u256'], *([cache_k] * SB_PPS), *([cache_v] * SB_PPS))


def _strict_upper(n):
    return jnp.asarray(np.tril(np.ones((n, n), np.float32), -1), BF16)


def _prepare_weights(w_in, conv_w, conv_b, conv_ln_g, conv_ln_b, mla_q_norm_g, mla_kv_norm_g, mla_w_uq, mla_w_uk,
                     mla_w_uv, mla_qk_norm_q, mla_qk_norm_k, gmlp_ln_g, gmlp_ln_b, gmlp_ws, gmlp_b, w_branch, w_out,
                     norm_ffn_g, router_group_w, router_group_b, router_expert_w, router_expert_b, moe_w_gate,
                     moe_w_up, moe_w_down, ple_norm_g, ple_w_gate, ple_w_proj, norm_mix_g, past_len):
    depth = w_in.shape[0]
    off, o = {}, 0
    for name, n in IN_SIZES:
        off[name] = (o, o + n)
        o += n
    seg = lambda name: w_in[:, :, off[name][0]:off[name][1]]
    zeros = lambda n: jnp.zeros((depth, D_MODEL, n), w_in.dtype)
    w_small = jnp.concatenate(
        [seg('conv'), seg('gmlp'), seg('mla_q'), seg('sb_q'), seg('mla_kv'), seg('sb_k'), seg('sb_v'),
         zeros(KR_LANE), seg('mla_kr'), zeros(LANES - KR_LANE - MLA_ROPE), zeros(Z_W - Z_KR - LANES)], axis=-1)
    w_gate = seg('gate').reshape(depth, D_MODEL, N_BRANCH, D_MODEL).transpose(0, 2, 1, 3)
    pad_heads = lambda w, d: jnp.pad(w.reshape(depth, w.shape[1], MLA_HEADS, d),
                                     ((0, 0), (0, 0), (0, 0), (0, HEAD_PAD - d))).reshape(depth, w.shape[1], -1)
    place_kr = np.zeros((LANES, MLA_HEADS * HEAD_PAD), np.float32)
    sel_rope = np.zeros((LANES, LANES), np.float32)
    for i in range(MLA_ROPE):
        sel_rope[KR_LANE + i, i] = 1.0
        for h in range(MLA_HEADS):
            place_kr[KR_LANE + i, h * HEAD_PAD + MLA_NOPE + i] = 1.0
    wk = jnp.concatenate([pad_heads(mla_w_uk, MLA_NOPE),
                          jnp.broadcast_to(jnp.asarray(place_kr), (depth,) + place_kr.shape)], axis=1)
    wukt = mla_w_uk.transpose(0, 2, 1)
    wukt_pad = jnp.pad(wukt.reshape(depth, MLA_HEADS, MLA_NOPE, MLA_KV_LORA),
                       ((0, 0), (0, 0), (0, HEAD_PAD - MLA_NOPE), (0, 0)))
    wbd = jnp.einsum('lchd,hg->lhcgd', mla_w_uv.reshape(depth, MLA_KV_LORA, MLA_HEADS, MLA_V),
                     jnp.eye(MLA_HEADS, dtype=mla_w_uv.dtype)).reshape(depth, MLA_HEADS * MLA_KV_LORA, 512)
    place_sb = np.zeros((512, SB_HEADS * LANES), np.float32)
    for h in range(SB_HEADS):
        for d in range(SB_HEAD_DIM):
            place_sb[h * SB_HEAD_DIM + d, h * LANES + (h // SB_REP) * SB_HEAD_DIM + d] = 1.0
    pad_gain = lambda g, s: jnp.pad(g * s, ((0, 0), (0, HEAD_PAD - MLA_QK)))[:, None, :]
    tril = jnp.tril(jnp.ones((CHUNK, CHUNK), bool))
    pc = past_len % CHUNK
    wr = jnp.concatenate([router_expert_w, router_group_w,
                          jnp.zeros((depth, D_MODEL, LANES - N_EXPERTS - N_GROUPS), F32)], axis=-1)
    wr_hi = wr.astype(BF16)
    wr_lo = (wr - wr_hi.astype(F32)).astype(BF16)
    r_bias = jnp.concatenate([router_expert_b, router_group_b,
                              jnp.zeros((depth, LANES - N_EXPERTS - N_GROUPS), F32)], axis=-1)[:, None, :]
    row = lambda v: v[:, None, :]
    return dict(
        w_small=w_small.astype(BF16), w_gate=w_gate.astype(BF16), g_mix=row(norm_mix_g),
        g_qn=row(mla_q_norm_g), g_kvn=row(mla_kv_norm_g),
        wuq=pad_heads(mla_w_uq, MLA_QK).astype(BF16), wk=wk.astype(BF16),
        gq=pad_gain(mla_qk_norm_q, MLA_SCALE), gk=pad_gain(mla_qk_norm_k, 1.0),
        wukt=wukt.astype(BF16), wukt_pad=wukt_pad.astype(BF16), wbd=wbd.astype(BF16),
        sel_rope=jnp.asarray(sel_rope, BF16), place_sb=jnp.asarray(place_sb, BF16),
        g_ln_g=row(gmlp_ln_g), g_ln_b=row(gmlp_ln_b),
        g_wtril=jnp.where(tril, gmlp_ws, 0.0).astype(BF16),
        g_bias=jnp.broadcast_to(gmlp_b[:, :, :, None], gmlp_b.shape + (CHUNK,)),
        g_wdiag=row(jnp.repeat(gmlp_ws[:, :, pc, pc], CHUNK, axis=-1)),
        g_brow=row(jnp.repeat(gmlp_b[:, :, pc], CHUNK, axis=-1)),
        conv_w=conv_w, conv_b=row(conv_b), conv_ln_g=row(conv_ln_g), conv_ln_b=row(conv_ln_b),
        w_branch=w_branch.astype(BF16), w_out=w_out.astype(BF16), g_ffn=row(norm_ffn_g),
        wr_hi=wr_hi, wr_lo=wr_lo, r_bias=r_bias,
        wgu=jnp.concatenate([moe_w_gate, moe_w_up], axis=-1).astype(BF16), wd=moe_w_down.astype(BF16),
        g_ple=row(ple_norm_g), w_ple_gate=ple_w_gate.astype(BF16), w_ple_proj=ple_w_proj.astype(BF16),
        u512=_strict_upper(512), u256=_strict_upper(SB_SUB * 128),
    )


def _rope_tables(pos):
    half = MLA_ROPE // 2
    inv = ROPE_BASE ** (-jnp.arange(half, dtype=F32) / half)
    ang = pos.astype(F32)[:, None] * inv[None, :]
    cos, sin = jnp.cos(ang), jnp.sin(ang)
    n = pos.shape[0]
    z = lambda w: jnp.zeros((n, w), F32)
    c_t = jnp.concatenate([jnp.ones((n, MLA_NOPE), F32), cos, cos, z(HEAD_PAD - MLA_QK)], axis=-1)
    sa_t = jnp.concatenate([z(MLA_NOPE + half), sin, z(HEAD_PAD - MLA_QK)], axis=-1)
    sb_t = jnp.concatenate([z(MLA_NOPE), -sin, z(half + HEAD_PAD - MLA_QK)], axis=-1)
    return c_t, sa_t, sb_t


def _tail(r, xn, branches, p_all, wts, layer, tm_mix, tm_out, tm_moe, tm_ple):
    mixed = gated_mix(xn, branches, wts, layer, tm_mix)
    r1, hn, comb = out_router(r, mixed, wts, layer, tm_out)
    ffn = moe_dense(hn, comb, wts, layer, tm_moe)
    return ple(r1, ffn, p_all, wts, layer, tm_ple)


def kernel(x_prompt, x_sample, cache_mla_ckv, cache_mla_kpe, cache_sb_k, cache_sb_v, state_conv, page_table, p_prompt, p_sample, norm_mix_g, w_in, conv_w, conv_b, conv_ln_g, conv_ln_b, mla_q_norm_g, mla_kv_norm_g, mla_w_uq, mla_w_uk, mla_w_uv, mla_qk_norm_q, mla_qk_norm_k, gmlp_ln_g, gmlp_ln_b, gmlp_ws, gmlp_b, w_branch, w_out, norm_ffn_g, router_group_w, router_group_b, router_expert_w, router_expert_b, moe_w_gate, moe_w_up, moe_w_down, ple_norm_g, ple_w_gate, ple_w_proj):
    n_batch, seq, d = x_prompt.shape
    nb, dec_seq, _ = x_sample.shape
    depth = w_in.shape[0]
    n_pages = page_table.shape[1]
    page = cache_mla_ckv.shape[2]
    past_len = n_pages * page
    assert d == D_MODEL and dec_seq == 1 and page == CHUNK
    assert seq % 512 == 0 and n_pages % MLA_PPS == 0 and n_pages % SB_PPS == 0

    wts = _prepare_weights(w_in, conv_w, conv_b, conv_ln_g, conv_ln_b, mla_q_norm_g, mla_kv_norm_g, mla_w_uq,
                           mla_w_uk, mla_w_uv, mla_qk_norm_q, mla_qk_norm_k, gmlp_ln_g, gmlp_ln_b, gmlp_ws, gmlp_b,
                           w_branch, w_out, norm_ffn_g, router_group_w, router_group_b, router_expert_w,
                           router_expert_b, moe_w_gate, moe_w_up, moe_w_down, ple_norm_g, ple_w_gate, ple_w_proj,
                           norm_mix_g, past_len)
    tabs_p = _rope_tables(jnp.tile(jnp.arange(seq), n_batch))
    tabs_s = _rope_tables(jnp.full((nb,), past_len, jnp.int32))
    page_flat = page_table.reshape(-1).astype(jnp.int32)
    cache_k2 = cache_sb_k.reshape(cache_sb_k.shape[:3] + (LANES,))
    cache_v2 = cache_sb_v.reshape(cache_sb_v.shape[:3] + (LANES,))
    p_p = p_prompt.reshape(depth, n_batch * seq, PLE_DIM)
    p_s = p_sample.reshape(depth, nb, PLE_DIM)

    rp = x_prompt.reshape(n_batch * seq, d)
    rs = x_sample.reshape(nb, d)
    st_p = [[] for _ in range(5)]
    st_s = [[] for _ in range(6)]
    for l in range(depth):
        zp, xnp = rms_matmul(rp, wts['g_mix'], wts['w_small'], l, 512, 768)
        a_p, q_p, k_p, ckv_p, kpe_p, sbq_p, outd_p, _ = branch_prep(zp, tabs_p, wts, l, True, 256)
        outa_p = conv_prompt(a_p, wts, l, n_batch, seq)
        outb_p = mla_prompt(q_p, k_p, ckv_p, wts, l, n_batch, seq)
        outc_p = sb_prompt(sbq_p, zp, wts, n_batch, seq)
        rp = _tail(rp, xnp, (outa_p, outb_p, outc_p, outd_p), p_p, wts, l, 1024, 256, 1024, 512)
        st_p[0].append(a_p.reshape(n_batch, seq, 512)[:, seq - (CONV_WIDTH - 1):])
        st_p[1].append(ckv_p.reshape(n_batch, seq, MLA_KV_LORA))
        st_p[2].append(kpe_p[:, KR_LANE:KR_LANE + MLA_ROPE].reshape(n_batch, seq, MLA_ROPE))
        st_p[3].append(zp[:, Z_SK:Z_SK + 128].reshape(n_batch, seq, SB_KV_HEADS, SB_HEAD_DIM))
        st_p[4].append(zp[:, Z_SV:Z_SV + 128].reshape(n_batch, seq, SB_KV_HEADS, SB_HEAD_DIM))
        zs, xns = rms_matmul(rs, wts['g_mix'], wts['w_small'], l, nb, 768)
        a_s, q_s, k_s, ckv_s, kpe_s, sbq_s, outd_s, vln_s = branch_prep(zs, tabs_s, wts, l, False, nb)
        outa_s = conv_sample(state_conv, a_s, wts, l)
        qabs, qrope = mla_sample_prep(q_s, wts, l)
        o_lat = mla_sample(q_s, k_s, qabs, qrope, ckv_s, cache_mla_ckv, cache_mla_kpe, page_flat, wts, l, n_pages)
        outb_s = latent_value(o_lat.reshape(nb, MLA_HEADS * MLA_KV_LORA), wts, l)
        outc_s = sb_sample(sbq_s, cache_k2, cache_v2, page_flat, wts, l, n_pages)
        outc_s = outc_s.reshape(nb, 512).astype(BF16)
        rs = _tail(rs, xns, (outa_s, outb_s, outc_s, outd_s), p_s, wts, l, nb, nb, nb, nb)
        st_s[0].append(jnp.concatenate([state_conv[l][:, 1:], a_s[:, None, :]], axis=1))
        st_s[1].append(ckv_s.reshape(nb, 1, MLA_KV_LORA))
        st_s[2].append(kpe_s[:, KR_LANE:KR_LANE + MLA_ROPE].reshape(nb, 1, MLA_ROPE))
        st_s[3].append(zs[:, Z_SK:Z_SK + 128].reshape(nb, 1, SB_KV_HEADS, SB_HEAD_DIM))
        st_s[4].append(zs[:, Z_SV:Z_SV + 128].reshape(nb, 1, SB_KV_HEADS, SB_HEAD_DIM))
        st_s[5].append(vln_s.reshape(nb, 1, 512))
    conv_state_prompt, mla_ckv_prompt, mla_kpe_prompt, sb_k_prompt, sb_v_prompt = [jnp.stack(a) for a in st_p]
    conv_state_sample, mla_ckv_sample, mla_kpe_sample, sb_k_sample, sb_v_sample, gmlp_v_sample = [
        jnp.stack(a) for a in st_s]
    return (rp.reshape(n_batch, seq, d), rs.reshape(nb, 1, d), conv_state_prompt, conv_state_sample,
            mla_ckv_prompt, mla_ckv_sample, mla_kpe_prompt, mla_kpe_sample, sb_k_prompt, sb_k_sample,
            sb_v_prompt, sb_v_sample, gmlp_v_sample)
```

```python
import functools
import math

import numpy as np
import jax
import jax.numpy as jnp
from jax import lax
from jax.experimental import pallas as pl
from jax.experimental.pallas import tpu as pltpu

F32 = jnp.float32
BF16 = jnp.bfloat16

D_MODEL = 2048
N_BRANCH = 4
MIX_W = 512
PLE_DIM = 256
EPS = 1e-6
CONV_WIDTH = 31
MLA_HEADS = 8
MLA_NOPE = 64
MLA_ROPE = 32
MLA_QK = MLA_NOPE + MLA_ROPE
MLA_V = 64
MLA_Q_LORA = 512
MLA_KV_LORA = 256
MLA_SCALE = MLA_QK ** -0.5
ROPE_BASE = 10000.0
SB_HEADS = 8
SB_KV_HEADS = 2
SB_REP = SB_HEADS // SB_KV_HEADS
SB_HEAD_DIM = 64
SB_SCALE = SB_HEAD_DIM ** -0.5
GMLP_GROUPS = 4
CHUNK = 128
N_GROUPS = 4
EXPERTS_PER_GROUP = 8
N_EXPERTS = N_GROUPS * EXPERTS_PER_GROUP
D_EXPERT = 256

LANES = 128
HEAD_PAD = LANES
NEG_BIG = -1e30
SB_DEAD = -104.0

Z_CONV, Z_GMLP, Z_MQ, Z_SQ, Z_MKV, Z_SK, Z_SV, Z_KR = 0, 1024, 2048, 2560, 3072, 3328, 3456, 3584
Z_W = 3840
KR_LANE = 64

IN_SIZES = (('conv', 1024), ('mla_q', 512), ('mla_kv', 256), ('mla_kr', 32), ('sb_q', 512), ('sb_k', 128),
            ('sb_v', 128), ('gmlp', 1024), ('gate', N_BRANCH * D_MODEL))


def _cp(n_axes, vmem_mb=None):
    return pltpu.CompilerParams(
        dimension_semantics=("arbitrary",) * n_axes,
        vmem_limit_bytes=None if vmem_mb is None else vmem_mb * 2 ** 20)


def _dot(a, b):
    return jnp.dot(a, b, preferred_element_type=F32)


def _dot_nt(a, b):
    return lax.dot_general(a, b, (((1,), (1,)), ((), ())), preferred_element_type=F32)


def _rms_rows(x, g):
    return x * lax.rsqrt(jnp.mean(x * x, axis=-1, keepdims=True) + EPS) * g


def _ln_rows(x, g, b):
    xc = x - jnp.mean(x, axis=-1, keepdims=True)
    return xc * lax.rsqrt(jnp.mean(xc * xc, axis=-1, keepdims=True) + EPS) * g + b


def _logsig(z):
    return jnp.minimum(z, 0.0) - jnp.log1p(jnp.exp(-jnp.abs(z)))


def _split_bf16(x):
    hi = x.astype(BF16)
    lo = (x - hi.astype(F32)).astype(BF16)
    return hi, lo


def _rms_matmul_body(x_ref, g_ref, w_ref, z_ref, xn_ref):
    @pl.when(pl.program_id(1) == 0)
    def _():
        xn_ref[...] = _rms_rows(x_ref[...], g_ref[...]).astype(BF16)

    z_ref[...] = _dot(xn_ref[...], w_ref[...])


def rms_matmul(x, g, w_all, layer, tm, tn):
    m, k = x.shape
    n = w_all.shape[-1]
    return pl.pallas_call(
        _rms_matmul_body,
        grid=(m // tm, n // tn),
        in_specs=[pl.BlockSpec((tm, k), lambda i, j: (i, 0)),
                  pl.BlockSpec((None, 1, k), lambda i, j: (layer, 0, 0)),
                  pl.BlockSpec((None, k, tn), lambda i, j: (layer, 0, j))],
        out_specs=[pl.BlockSpec((tm, tn), lambda i, j: (i, j)),
                   pl.BlockSpec((tm, k), lambda i, j: (i, 0))],
        out_shape=[jax.ShapeDtypeStruct((m, n), F32), jax.ShapeDtypeStruct((m, k), BF16)],
        compiler_params=_cp(2, 48),
        name="rms_matmul",
    )(x, g, w_all)


def _rope_slab(x, c, sa, sb):
    return x * c + pltpu.roll(x, 16, 1) * sa + pltpu.roll(x, LANES - 16, 1) * sb


def _prep_body(z_ref, c_ref, sa_ref, sb_ref, gqn_ref, gkvn_ref, wuq_ref, wk_ref, gq_ref, gk_ref, place_ref,
               glg_ref, glb_ref, gw_ref, gb_ref,
               a_ref, q_ref, k_ref, ckv_ref, kpe_ref, sbq_ref, outd_ref, vln_ref, *, is_prompt, tm):
    c, sa, sb = c_ref[...], sa_ref[...], sb_ref[...]
    a_ref[...] = z_ref[:, Z_CONV:Z_CONV + 512] * jax.nn.sigmoid(z_ref[:, Z_CONV + 512:Z_CONV + 1024])
    cq = _rms_rows(z_ref[:, Z_MQ:Z_MQ + 512], gqn_ref[...]).astype(BF16)
    ckv = _rms_rows(z_ref[:, Z_MKV:Z_MKV + 256], gkvn_ref[...])
    ckv_ref[...] = ckv
    kpe = _rope_slab(z_ref[:, Z_KR:Z_KR + LANES], c, sa, sb)
    kpe_ref[...] = kpe
    q0 = _dot(cq, wuq_ref[...])
    kcat = jnp.concatenate([ckv, kpe], axis=-1).astype(BF16)
    k0 = _dot(kcat, wk_ref[...])
    gq, gk = gq_ref[...], gk_ref[...]
    for h in range(MLA_HEADS):
        sl = slice(h * HEAD_PAD, (h + 1) * HEAD_PAD)
        qh = _rope_slab(q0[:, sl], c, sa, sb)
        qh = qh * lax.rsqrt(jnp.sum(qh * qh, -1, keepdims=True) * (1.0 / MLA_QK) + EPS) * gq
        q_ref[:, sl] = qh.astype(BF16)
        kh = k0[:, sl]
        kh = kh * lax.rsqrt(jnp.sum(kh * kh, -1, keepdims=True) * (1.0 / MLA_QK) + EPS) * gk
        k_ref[:, sl] = kh.astype(BF16)
    sq = (z_ref[:, Z_SQ:Z_SQ + 512] * SB_SCALE).astype(BF16)
    sbq_ref[...] = _dot(sq, place_ref[...]).astype(BF16)
    ge = jax.nn.gelu(z_ref[:, Z_GMLP:Z_GMLP + 1024])
    u = ge[:, :512]
    vln = _ln_rows(ge[:, 512:], glg_ref[...], glb_ref[...])
    vln_ref[...] = vln
    if is_prompt:
        vb = vln.astype(BF16)
        for ch in range(tm // CHUNK):
            rs = slice(ch * CHUNK, (ch + 1) * CHUNK)
            for g in range(GMLP_GROUPS):
                cs = slice(g * 128, (g + 1) * 128)
                mix = _dot(gw_ref[g], vb[rs, cs]) + gb_ref[g]
                outd_ref[rs, cs] = (u[rs, cs] * mix).astype(BF16)
    else:
        outd_ref[...] = (u * (vln * gw_ref[...] + gb_ref[...])).astype(BF16)


def branch_prep(z, tabs, wts, layer, is_prompt, tm):
    m = z.shape[0]
    c_t, sa_t, sb_t = tabs
    row = lambda i: (i, 0)
    lay3 = lambda i: (layer, 0, 0)
    if is_prompt:
        gw, gb = wts['g_wtril'], wts['g_bias']
        gw_spec = pl.BlockSpec((None, GMLP_GROUPS, CHUNK, CHUNK), lambda i: (layer, 0, 0, 0))
        gb_spec = pl.BlockSpec((None, GMLP_GROUPS, CHUNK, CHUNK), lambda i: (layer, 0, 0, 0))
    else:
        gw, gb = wts['g_wdiag'], wts['g_brow']
        gw_spec = pl.BlockSpec((None, 1, 512), lay3)
        gb_spec = pl.BlockSpec((None, 1, 512), lay3)
    in_specs = [
        pl.BlockSpec((tm, Z_W), row),
        pl.BlockSpec((tm, LANES), row), pl.BlockSpec((tm, LANES), row), pl.BlockSpec((tm, LANES), row),
        pl.BlockSpec((None, 1, 512), lay3), pl.BlockSpec((None, 1, 256), lay3),
        pl.BlockSpec((None, 512, 1024), lay3), pl.BlockSpec((None, 384, 1024), lay3),
        pl.BlockSpec((None, 1, LANES), lay3), pl.BlockSpec((None, 1, LANES), lay3),
        pl.BlockSpec((512, 1024), lambda i: (0, 0)),
        pl.BlockSpec((None, 1, 512), lay3), pl.BlockSpec((None, 1, 512), lay3),
        gw_spec, gb_spec,
    ]
    outs = [(512, F32), (1024, BF16), (1024, BF16), (256, F32), (LANES, F32), (1024, BF16), (512, BF16), (512, F32)]
    return pl.pallas_call(
        functools.partial(_prep_body, is_prompt=is_prompt, tm=tm),
        grid=(m // tm,),
        in_specs=in_specs,
        out_specs=[pl.BlockSpec((tm, w), row) for w, _ in outs],
        out_shape=[jax.ShapeDtypeStruct((m, w), dt) for w, dt in outs],
        compiler_params=_cp(1, 48),
        name="branch_prep_p" if is_prompt else "branch_prep_s",
    )(z, c_t, sa_t, sb_t, wts['g_qn'], wts['g_kvn'], wts['wuq'], wts['wk'], wts['gq'], wts['gk'], wts['place_sb'],
      wts['g_ln_g'], wts['g_ln_b'], gw, gb)


CONV_HALO = 32


def _conv_p_body(a_ref, w_ref, b_ref, g_ref, bb_ref, o_ref, buf_ref, y_ref, *, tl):
    @pl.when(pl.program_id(1) == 0)
    def _():
        buf_ref[0:CONV_HALO, :] = jnp.zeros((CONV_HALO, 512), F32)

    buf_ref[CONV_HALO:CONV_HALO + tl, :] = a_ref[...]
    rb = 128
    shift = CONV_HALO - (CONV_WIDTH - 1)
    for r in range(tl // rb):
        for cb in range(4):
            cs = slice(cb * LANES, (cb + 1) * LANES)
            acc = jnp.zeros((rb, LANES), F32)
            for j in range(CONV_WIDTH):
                lo = r * rb + j + shift
                acc = acc + w_ref[j:j + 1, cs] * buf_ref[lo:lo + rb, cs]
            y_ref[r * rb:(r + 1) * rb, cs] = acc
    y = _ln_rows(y_ref[...] + b_ref[...], g_ref[...], bb_ref[...])
    o_ref[...] = jax.nn.silu(y).astype(BF16)
    buf_ref[0:CONV_HALO, :] = buf_ref[tl:tl + CONV_HALO, :]


def conv_prompt(a, wts, layer, n_batch, seq, tl=256):
    nt = seq // tl
    lay3 = lambda b, i: (layer, 0, 0)
    return pl.pallas_call(
        functools.partial(_conv_p_body, tl=tl),
        grid=(n_batch, nt),
        in_specs=[pl.BlockSpec((tl, 512), lambda b, i: (b * nt + i, 0)),
                  pl.BlockSpec((None, CONV_WIDTH, 512), lay3),
                  pl.BlockSpec((None, 1, 512), lay3), pl.BlockSpec((None, 1, 512), lay3),
                  pl.BlockSpec((None, 1, 512), lay3)],
        out_specs=pl.BlockSpec((tl, 512), lambda b, i: (b * nt + i, 0)),
        out_shape=jax.ShapeDtypeStruct((n_batch * seq, 512), BF16),
        scratch_shapes=[pltpu.VMEM((tl + CONV_HALO, 512), F32), pltpu.VMEM((tl, 512), F32)],
        compiler_params=_cp(2),
        name="conv_prompt",
    )(a, wts['conv_w'], wts['conv_b'], wts['conv_ln_g'], wts['conv_ln_b'])


def _conv_s_body(st_ref, a_ref, w_ref, b_ref, g_ref, bb_ref, o_ref):
    y = a_ref[...] * w_ref[CONV_WIDTH - 1:CONV_WIDTH, :] + b_ref[...]
    for j in range(CONV_WIDTH - 1):
        y = y + st_ref[j] * w_ref[j:j + 1, :]
    o_ref[...] = jax.nn.silu(_ln_rows(y, g_ref[...], bb_ref[...])).astype(BF16)


def conv_sample(state_t, a, wts, layer):
    nb = a.shape[0]
    lay3 = lambda i: (layer, 0, 0)
    return pl.pallas_call(
        _conv_s_body,
        grid=(1,),
        in_specs=[pl.BlockSpec((None, CONV_WIDTH - 1, nb, 512), lambda i: (layer, 0, 0, 0)),
                  pl.BlockSpec((nb, 512), lambda i: (0, 0)),
                  pl.BlockSpec((None, CONV_WIDTH, 512), lay3),
                  pl.BlockSpec((None, 1, 512), lay3), pl.BlockSpec((None, 1, 512), lay3),
                  pl.BlockSpec((None, 1, 512), lay3)],
        out_specs=pl.BlockSpec((nb, 512), lambda i: (0, 0)),
        out_shape=jax.ShapeDtypeStruct((nb, 512), BF16),
        compiler_params=_cp(1, 48),
        name="conv_sample",
    )(state_t, a, wts['conv_w'], wts['conv_b'], wts['conv_ln_g'], wts['conv_ln_b'])


def _pair_tables(n_tiles, descending):
    qs, ks = [], []
    for qi in range(n_tiles):
        kis = range(qi, -1, -1) if descending else range(qi + 1)
        for ki in kis:
            qs.append(qi)
            ks.append(ki)
    return np.asarray(qs, np.int32), np.asarray(ks, np.int32)


def _mla_flash_body(qt_ref, kt_ref, q_ref, k_ref, c_ref, wbd_ref, o_ref, m_ref, l_ref, acc_ref, *, tq, tk):
    s_idx = pl.program_id(1)
    qi, ki = qt_ref[s_idx], kt_ref[s_idx]

    @pl.when(ki == 0)
    def _():
        m_ref[...] = jnp.full(m_ref.shape, NEG_BIG, F32)
        l_ref[...] = jnp.zeros(l_ref.shape, F32)
        acc_ref[...] = jnp.zeros(acc_ref.shape, F32)

    def step(masked):
        c = c_ref[...].astype(BF16)
        if masked:
            allow = (lax.broadcasted_iota(jnp.int32, (tq, tk), 1) <= lax.broadcasted_iota(jnp.int32, (tq, tk), 0))
        for h in range(MLA_HEADS):
            sl = slice(h * HEAD_PAD, (h + 1) * HEAD_PAD)
            s = _dot_nt(q_ref[:, sl], k_ref[:, sl])
            if masked:
                s = jnp.where(allow, s, NEG_BIG)
            m_prev = m_ref[h]
            m_new = jnp.maximum(m_prev, jnp.max(s, -1, keepdims=True))
            alpha = jnp.exp(m_prev - m_new)
            p = jnp.exp(s - m_new[:, :1])
            l_ref[h] = alpha * l_ref[h] + jnp.sum(p, -1, keepdims=True)
            acc_ref[h] = alpha[:, :1] * acc_ref[h] + _dot(p.astype(BF16), c)
            m_ref[h] = m_new

    @pl.when(ki < qi)
    def _():
        step(False)

    @pl.when(ki == qi)
    def _():
        step(True)
        o = jnp.concatenate(
            [(acc_ref[h] * (1.0 / l_ref[h][:, :1])).astype(BF16) for h in range(MLA_HEADS)], axis=-1)
        o_ref[...] = _dot(o, wbd_ref[...]).astype(BF16)


def mla_prompt(q, k, ckv, wts, layer, n_batch, seq, tq=512):
    nt = seq // tq
    qs, ks = _pair_tables(nt, descending=False)
    qmap = lambda b, s, qt, kt: (b * nt + qt[s], 0)
    kmap = lambda b, s, qt, kt: (b * nt + kt[s], 0)
    grid_spec = pltpu.PrefetchScalarGridSpec(
        num_scalar_prefetch=2,
        grid=(n_batch, len(qs)),
        in_specs=[pl.BlockSpec((tq, 1024), qmap), pl.BlockSpec((tq, 1024), kmap),
                  pl.BlockSpec((tq, MLA_KV_LORA), kmap),
                  pl.BlockSpec((None, MLA_HEADS * MLA_KV_LORA, 512), lambda b, s, qt, kt: (layer, 0, 0))],
        out_specs=pl.BlockSpec((tq, 512), qmap),
        scratch_shapes=[pltpu.VMEM((MLA_HEADS, tq, LANES), F32), pltpu.VMEM((MLA_HEADS, tq, LANES), F32),
                        pltpu.VMEM((MLA_HEADS, tq, MLA_KV_LORA), F32)],
    )
    return pl.pallas_call(
        functools.partial(_mla_flash_body, tq=tq, tk=tq),
        grid_spec=grid_spec,
        out_shape=jax.ShapeDtypeStruct((n_batch * seq, 512), BF16),
        compiler_params=_cp(2, 48),
        name="mla_prompt",
    )(jnp.asarray(qs), jnp.asarray(ks), q, k, ckv, wts['wbd'])


def _sb_tile(z, carry, u_mat, v_bf, allow, v_transposed=False):
    lb = _logsig(z)
    lk = lb - z
    if allow is not None:
        lk = jnp.where(allow, lk, 0.0)
    hi, lo = _split_bf16(lk)
    later = carry + _dot(hi, u_mat) + _dot(lo, u_mat)
    a = jnp.exp(lb + later)
    if allow is not None:
        a = jnp.where(allow, a, 0.0)
    pv = _dot_nt(a.astype(BF16), v_bf) if v_transposed else _dot(a.astype(BF16), v_bf)
    return pv, carry + jnp.sum(lk, -1, keepdims=True)


def _pick_group_half(acc_lo, acc_hi, group):
    lane = lax.broadcasted_iota(jnp.int32, acc_lo.shape, 1)
    if group == 0:
        return jnp.where(lane < 64, acc_lo, pltpu.roll(acc_hi, 64, 1))
    return jnp.where(lane < 64, pltpu.roll(acc_lo, 64, 1), acc_hi)


def _sb_flash_body(qt_ref, kt_ref, q_ref, k_ref, v_ref, u_ref, o_ref, carry_ref, acc_ref, alive_ref, *, tq, tk):
    s_idx = pl.program_id(1)
    qi, ki = qt_ref[s_idx], kt_ref[s_idx]

    @pl.when(ki == qi)
    def _():
        carry_ref[...] = jnp.zeros(carry_ref.shape, F32)
        acc_ref[...] = jnp.zeros(acc_ref.shape, F32)
        alive_ref[0] = 1

    def step(masked):
        k_bf = k_ref[...].astype(BF16)
        v_bf = v_ref[...].astype(BF16)
        u_mat = u_ref[...]
        allow = None
        if masked:
            allow = (lax.broadcasted_iota(jnp.int32, (tq, tk), 1) < lax.broadcasted_iota(jnp.int32, (tq, tk), 0))
        worst = None
        for h in range(SB_HEADS):
            z = _dot_nt(q_ref[:, h * LANES:(h + 1) * LANES], k_bf)
            pv, carry = _sb_tile(z, carry_ref[h][:, :1], u_mat, v_bf, allow)
            acc_ref[h] = acc_ref[h] + pv
            carry_ref[h] = jnp.broadcast_to(carry, (tq, LANES))
            hmax = jnp.max(carry)
            worst = hmax if worst is None else jnp.maximum(worst, hmax)
        alive_ref[0] = (worst >= SB_DEAD).astype(jnp.int32)

    @pl.when(jnp.logical_and(ki == qi, alive_ref[0] == 1))
    def _():
        step(True)

    @pl.when(jnp.logical_and(ki < qi, alive_ref[0] == 1))
    def _():
        step(False)

    @pl.when(ki == 0)
    def _():
        for m in range(SB_HEADS // 2):
            blk = _pick_group_half(acc_ref[2 * m], acc_ref[2 * m + 1], (2 * m) // SB_REP)
            o_ref[:, m * LANES:(m + 1) * LANES] = blk.astype(BF16)


def sb_prompt(sbq, z, wts, n_batch, seq, tq=512):
    nt = seq // tq
    qs, ks = _pair_tables(nt, descending=True)
    kcol, vcol = Z_SK // LANES, Z_SV // LANES
    grid_spec = pltpu.PrefetchScalarGridSpec(
        num_scalar_prefetch=2,
        grid=(n_batch, len(qs)),
        in_specs=[pl.BlockSpec((tq, 1024), lambda b, s, qt, kt: (b * nt + qt[s], 0)),
                  pl.BlockSpec((tq, LANES), lambda b, s, qt, kt: (b * nt + kt[s], kcol)),
                  pl.BlockSpec((tq, LANES), lambda b, s, qt, kt: (b * nt + kt[s], vcol)),
                  pl.BlockSpec((tq, tq), lambda b, s, qt, kt: (0, 0))],
        out_specs=pl.BlockSpec((tq, 512), lambda b, s, qt, kt: (b * nt + qt[s], 0)),
        scratch_shapes=[pltpu.VMEM((SB_HEADS, tq, LANES), F32), pltpu.VMEM((SB_HEADS, tq, LANES), F32),
                        pltpu.SMEM((1,), jnp.int32)],
    )
    return pl.pallas_call(
        functools.partial(_sb_flash_body, tq=tq, tk=tq),
        grid_spec=grid_spec,
        out_shape=jax.ShapeDtypeStruct((n_batch * seq, 512), BF16),
        compiler_params=_cp(2, 48),
        name="sb_prompt",
    )(jnp.asarray(qs), jnp.asarray(ks), sbq, z, z, wts['u512'])


def _mix_body(xn_ref, a_ref, b_ref, c_ref, d_ref, wg_ref, wb_ref, o_ref):
    xn = xn_ref[...]
    acc = None
    for i, br in enumerate((a_ref, b_ref, c_ref, d_ref)):
        term = jax.nn.sigmoid(_dot(xn, wg_ref[i])) * _dot(br[...], wb_ref[i])
        acc = term if acc is None else acc + term
    o_ref[...] = acc.astype(BF16)


def gated_mix(xn, branches, wts, layer, tm, tn=256):
    m = xn.shape[0]
    row = lambda i, j: (i, 0)
    return pl.pallas_call(
        _mix_body,
        grid=(m // tm, D_MODEL // tn),
        in_specs=[pl.BlockSpec((tm, D_MODEL), row)] + [pl.BlockSpec((tm, MIX_W), row)] * 4 + [
            pl.BlockSpec((None, N_BRANCH, D_MODEL, tn), lambda i, j: (layer, 0, 0, j)),
            pl.BlockSpec((None, N_BRANCH, MIX_W, tn), lambda i, j: (layer, 0, 0, j))],
        out_specs=pl.BlockSpec((tm, tn), lambda i, j: (i, j)),
        out_shape=jax.ShapeDtypeStruct((m, D_MODEL), BF16),
        compiler_params=_cp(2, 48),
        name="gated_mix",
    )(xn, *branches, wts['w_gate'], wts['w_branch'])


def _route(logits):
    lane = lax.broadcasted_iota(jnp.int32, logits.shape, 1)
    lane_f = lane.astype(F32)
    far = float(LANES)
    gmask = (lane >> 2) == (N_EXPERTS // 4)
    gl = jnp.where(gmask, logits, NEG_BIG)
    gmax = jnp.max(gl, -1, keepdims=True)
    g_p = 1.0 / jnp.sum(jnp.where(gmask, jnp.exp(gl - gmax), 0.0), -1, keepdims=True)
    gidx = jnp.min(jnp.where(gl == gmax, lane_f, far), -1, keepdims=True) - float(N_EXPERTS)
    el = jnp.where((lane >> 3).astype(F32) == gidx, logits, NEG_BIG)
    m1 = jnp.max(el, -1, keepdims=True)
    i1 = jnp.min(jnp.where(el == m1, lane_f, far), -1, keepdims=True)
    el2 = jnp.where(lane_f == i1, NEG_BIG, el)
    m2 = jnp.max(el2, -1, keepdims=True)
    i2 = jnp.min(jnp.where(el2 == m2, lane_f, far), -1, keepdims=True)
    e2 = jnp.exp(m2 - m1)
    inv = 1.0 / (1.0 + e2)
    return jnp.where(lane_f == i1, inv * g_p, 0.0) + jnp.where(lane_f == i2, e2 * inv * g_p, 0.0)


def _out_router_body(r_ref, mix_ref, wo_ref, g_ref, whi_ref, wlo_ref, rb_ref, r1_ref, hn_ref, comb_ref):
    r1 = r_ref[...] + _dot(mix_ref[...], wo_ref[...])
    r1_ref[...] = r1
    h = _rms_rows(r1, g_ref[...])
    hi, lo = _split_bf16(h)
    hn_ref[...] = hi
    logits = _dot(hi, whi_ref[...]) + _dot(lo, whi_ref[...]) + _dot(hi, wlo_ref[...]) + rb_ref[...]
    comb_ref[...] = _route(logits)


def out_router(r, mixed, wts, layer, tm):
    m = r.shape[0]
    row = lambda i: (i, 0)
    lay3 = lambda i: (layer, 0, 0)
    return pl.pallas_call(
        _out_router_body,
        grid=(m // tm,),
        in_specs=[pl.BlockSpec((tm, D_MODEL), row), pl.BlockSpec((tm, D_MODEL), row),
                  pl.BlockSpec((None, D_MODEL, D_MODEL), lay3), pl.BlockSpec((None, 1, D_MODEL), lay3),
                  pl.BlockSpec((None, D_MODEL, LANES), lay3), pl.BlockSpec((None, D_MODEL, LANES), lay3),
                  pl.BlockSpec((None, 1, LANES), lay3)],
        out_specs=[pl.BlockSpec((tm, D_MODEL), row), pl.BlockSpec((tm, D_MODEL), row),
                   pl.BlockSpec((tm, LANES), row)],
        out_shape=[jax.ShapeDtypeStruct((m, D_MODEL), F32), jax.ShapeDtypeStruct((m, D_MODEL), BF16),
                   jax.ShapeDtypeStruct((m, LANES), F32)],
        compiler_params=_cp(1, 48),
        name="out_router",
    )(r, mixed, wts['w_out'], wts['g_ffn'], wts['wr_hi'], wts['wr_lo'], wts['r_bias'])


def _moe_body(hn_ref, comb_ref, wgu_ref, wd_ref, o_ref):
    e = pl.program_id(1)

    @pl.when(e == 0)
    def _():
        o_ref[...] = jnp.zeros(o_ref.shape, F32)

    gu = _dot(hn_ref[...], wgu_ref[...])
    comb = comb_ref[...]
    lane = lax.broadcasted_iota(jnp.int32, comb.shape, 1)
    ce = jnp.sum(jnp.where(lane == e, comb, 0.0), -1, keepdims=True)
    hid = jax.nn.silu(gu[:, :D_EXPERT]) * gu[:, D_EXPERT:] * ce
    o_ref[...] += _dot(hid.astype(BF16), wd_ref[...])


def moe_dense(hn, comb, wts, layer, tm):
    m = hn.shape[0]
    row = lambda i, e: (i, 0)
    return pl.pallas_call(
        _moe_body,
        grid=(m // tm, N_EXPERTS),
        in_specs=[pl.BlockSpec((tm, D_MODEL), row), pl.BlockSpec((tm, LANES), row),
                  pl.BlockSpec((None, None, D_MODEL, 2 * D_EXPERT), lambda i, e: (layer, e, 0, 0)),
                  pl.BlockSpec((None, None, D_EXPERT, D_MODEL), lambda i, e: (layer, e, 0, 0))],
        out_specs=pl.BlockSpec((tm, D_MODEL), row),
        out_shape=jax.ShapeDtypeStruct((m, D_MODEL), F32),
        compiler_params=_cp(2, 48),
        name="moe_dense",
    )(hn, comb, wts['wgu'], wts['wd'])


def _ple_body(r_ref, f_ref, rt_ref, ft_ref, p_ref, g_ref, wg_ref, wp_ref, o_ref, xn_ref):
    @pl.when(pl.program_id(1) == 0)
    def _():
        xn_ref[...] = _rms_rows(r_ref[...] + f_ref[...], g_ref[...]).astype(BF16)

    gate = jax.nn.sigmoid(_dot(xn_ref[...], wg_ref[...]))
    o_ref[...] = rt_ref[...] + ft_ref[...] + gate * _dot(p_ref[...].astype(BF16), wp_ref[...])


def ple(r, ffn, p_all, wts, layer, tm, tn=512):
    m = r.shape[0]
    return pl.pallas_call(
        _ple_body,
        grid=(m // tm, D_MODEL // tn),
        in_specs=[pl.BlockSpec((tm, D_MODEL), lambda i, j: (i, 0)), pl.BlockSpec((tm, D_MODEL), lambda i, j: (i, 0)),
                  pl.BlockSpec((tm, tn), lambda i, j: (i, j)), pl.BlockSpec((tm, tn), lambda i, j: (i, j)),
                  pl.BlockSpec((None, tm, PLE_DIM), lambda i, j: (layer, i, 0)),
                  pl.BlockSpec((None, 1, D_MODEL), lambda i, j: (layer, 0, 0)),
                  pl.BlockSpec((None, D_MODEL, tn), lambda i, j: (layer, 0, j)),
                  pl.BlockSpec((None, PLE_DIM, tn), lambda i, j: (layer, 0, j))],
        out_specs=pl.BlockSpec((tm, tn), lambda i, j: (i, j)),
        out_shape=jax.ShapeDtypeStruct((m, D_MODEL), F32),
        scratch_shapes=[pltpu.VMEM((tm, D_MODEL), BF16)],
        compiler_params=_cp(2, 48),
        name="ple",
    )(r, ffn, r, ffn, p_all, wts['g_ple'], wts['w_ple_gate'], wts['w_ple_proj'])


def _mla_sprep_body(q_ref, gk_ref, wukt_ref, sel_ref, qabs_ref, qrope_ref):
    gk = gk_ref[...]
    for h in range(MLA_HEADS):
        sl = slice(h * HEAD_PAD, (h + 1) * HEAD_PAD)
        qg = (q_ref[:, sl].astype(F32) * gk).astype(BF16)
        qabs_ref[:, h * MLA_KV_LORA:(h + 1) * MLA_KV_LORA] = _dot(qg, wukt_ref[h]).astype(BF16)
        qrope_ref[:, sl] = _dot(qg, sel_ref[...]).astype(BF16)


def mla_sample_prep(q, wts, layer):
    nb = q.shape[0]
    return pl.pallas_call(
        _mla_sprep_body,
        grid=(1,),
        in_specs=[pl.BlockSpec((nb, 1024), lambda i: (0, 0)),
                  pl.BlockSpec((None, 1, LANES), lambda i: (layer, 0, 0)),
                  pl.BlockSpec((None, MLA_HEADS, HEAD_PAD, MLA_KV_LORA), lambda i: (layer, 0, 0, 0)),
                  pl.BlockSpec((LANES, LANES), lambda i: (0, 0))],
        out_specs=[pl.BlockSpec((nb, MLA_HEADS * MLA_KV_LORA), lambda i: (0, 0)),
                   pl.BlockSpec((nb, 1024), lambda i: (0, 0))],
        out_shape=[jax.ShapeDtypeStruct((nb, MLA_HEADS * MLA_KV_LORA), BF16),
                   jax.ShapeDtypeStruct((nb, 1024), BF16)],
        compiler_params=_cp(1),
        name="mla_sample_prep",
    )(q, wts['gk'], wts['wukt_pad'], wts['sel_rope'])


MLA_PPS = 32
MLA_SUB = 4


def _mla_dec_body(pt_ref, q8_ref, k8_ref, qabs_ref, qrope_ref, cnew_ref, wukt_ref, *rest, n_chunks):
    ckv_refs = rest[:MLA_PPS]
    kpe_refs = rest[MLA_PPS:2 * MLA_PPS]
    o_ref, m_ref, l_ref, acc_ref = rest[2 * MLA_PPS:]
    ci = pl.program_id(1)

    @pl.when(ci == 0)
    def _():
        m_ref[...] = jnp.full(m_ref.shape, NEG_BIG, F32)
        l_ref[...] = jnp.zeros(l_ref.shape, F32)
        acc_ref[...] = jnp.zeros(acc_ref.shape, F32)

    qabs = qabs_ref[...]
    qrope = qrope_ref[:, :MLA_ROPE]
    wukt = wukt_ref[...]
    s_parts, c_parts = [], []
    for sub in range(MLA_PPS // MLA_SUB):
        pages = range(sub * MLA_SUB, (sub + 1) * MLA_SUB)
        c = jnp.concatenate([ckv_refs[i][...] for i in pages], axis=0).astype(BF16)
        kpt = jnp.concatenate([kpe_refs[i][...] for i in pages], axis=1)
        tk = c.shape[0]
        kt = _dot_nt(wukt, c)
        ssn = jnp.sum((kt * kt).reshape(MLA_HEADS, MLA_NOPE, tk), axis=1)
        ssr = jnp.sum(kpt * kpt, axis=0, keepdims=True)
        raw = _dot_nt(qabs, c) + _dot(qrope, kpt.astype(BF16))
        s_parts.append(raw * lax.rsqrt((ssn + ssr) * (1.0 / MLA_QK) + EPS))
        c_parts.append(c)
    s = jnp.concatenate(s_parts, axis=1)
    c_all = jnp.concatenate(c_parts, axis=0)
    m_prev = m_ref[...]
    m_new = jnp.maximum(m_prev, jnp.max(s, -1, keepdims=True))
    alpha = jnp.exp(m_prev - m_new)
    p = jnp.exp(s - m_new[:, :1])
    l_ref[...] = alpha * l_ref[...] + jnp.sum(p, -1, keepdims=True)
    acc_ref[...] = alpha[:, :1] * acc_ref[...] + _dot(p.astype(BF16), c_all)
    m_ref[...] = m_new

    @pl.when(ci == n_chunks - 1)
    def _():
        s_self = jnp.sum(q8_ref[...].astype(F32) * k8_ref[...].astype(F32), -1, keepdims=True)
        m_prev = m_ref[...]
        m_fin = jnp.maximum(m_prev, s_self)
        alpha = jnp.exp(m_prev - m_fin)
        p_self = jnp.exp(s_self - m_fin[:, :1])
        l_fin = alpha[:, :1] * l_ref[:, :1] + p_self
        acc = alpha[:, :1] * acc_ref[...] + p_self * cnew_ref[...]
        o_ref[...] = acc * (1.0 / l_fin)


def mla_sample(q, k, qabs, qrope, ckv_new, cache_ckv, cache_kpe_t, page_flat, wts, layer, n_pages):
    nb = q.shape[0]
    n_chunks = n_pages // MLA_PPS
    page = cache_ckv.shape[2]
    seq3 = lambda b, c, pt: (b, 0, 0)
    page_map = lambda i: (lambda b, c, pt: (layer, pt[b * n_pages + c * MLA_PPS + i], 0, 0))

    grid_spec = pltpu.PrefetchScalarGridSpec(
        num_scalar_prefetch=1,
        grid=(nb, n_chunks),
        in_specs=[pl.BlockSpec((None, MLA_HEADS, HEAD_PAD), seq3), pl.BlockSpec((None, MLA_HEADS, HEAD_PAD), seq3),
                  pl.BlockSpec((None, MLA_HEADS, MLA_KV_LORA), seq3), pl.BlockSpec((None, MLA_HEADS, HEAD_PAD), seq3),
                  pl.BlockSpec((None, 1, MLA_KV_LORA), seq3),
                  pl.BlockSpec((None, MLA_HEADS * MLA_NOPE, MLA_KV_LORA), lambda b, c, pt: (layer, 0, 0))]
        + [pl.BlockSpec((None, None, page, MLA_KV_LORA), page_map(i)) for i in range(MLA_PPS)]
        + [pl.BlockSpec((None, None, MLA_ROPE, page), page_map(i)) for i in range(MLA_PPS)],
        out_specs=pl.BlockSpec((None, MLA_HEADS, MLA_KV_LORA), seq3),
        scratch_shapes=[pltpu.VMEM((MLA_HEADS, LANES), F32), pltpu.VMEM((MLA_HEADS, LANES), F32),
                        pltpu.VMEM((MLA_HEADS, MLA_KV_LORA), F32)],
    )
    return pl.pallas_call(
        functools.partial(_mla_dec_body, n_chunks=n_chunks),
        grid_spec=grid_spec,
        out_shape=jax.ShapeDtypeStruct((nb, MLA_HEADS, MLA_KV_LORA), F32),
        compiler_params=_cp(2, 48),
        name="mla_sample",
    )(page_flat, q.reshape(nb, MLA_HEADS, HEAD_PAD), k.reshape(nb, MLA_HEADS, HEAD_PAD),
      qabs.reshape(nb, MLA_HEADS, MLA_KV_LORA), qrope.reshape(nb, MLA_HEADS, HEAD_PAD),
      ckv_new.reshape(nb, 1, MLA_KV_LORA), wts['wukt'], *([cache_ckv] * MLA_PPS), *([cache_kpe_t] * MLA_PPS))


def _latent_value_body(o_ref, wbd_ref, out_ref):
    out_ref[...] = _dot(o_ref[...].astype(BF16), wbd_ref[...]).astype(BF16)


def latent_value(o_lat, wts, layer):
    nb = o_lat.shape[0]
    return pl.pallas_call(
        _latent_value_body,
        grid=(1,),
        in_specs=[pl.BlockSpec((nb, MLA_HEADS * MLA_KV_LORA), lambda i: (0, 0)),
                  pl.BlockSpec((None, MLA_HEADS * MLA_KV_LORA, 512), lambda i: (layer, 0, 0))],
        out_specs=pl.BlockSpec((nb, 512), lambda i: (0, 0)),
        out_shape=jax.ShapeDtypeStruct((nb, 512), BF16),
        compiler_params=_cp(1),
        name="latent_value",
    )(o_lat, wts['wbd'])


SB_HEAD_PAGES = 4
SB_BLK = 2


def _sb_dec_body(pt_ref, q_ref, u_ref, kt_hbm, vt_hbm, *rest, layer, n_pages):
    k_refs = rest[:SB_HEAD_PAGES]
    v_refs = rest[SB_HEAD_PAGES:2 * SB_HEAD_PAGES]
    o_ref, kbuf, vbuf, sems, carry_ref, acc_ref, alive_ref = rest[2 * SB_HEAD_PAGES:]
    b = pl.program_id(0)
    q = q_ref[...]
    u_mat = u_ref[...]

    def block(kt, vt, carry, acc):
        pv, carry = _sb_tile(_dot(q, kt.astype(BF16)), carry, u_mat, vt.astype(BF16), None, v_transposed=True)
        return carry, acc + pv

    def publish(carry, acc):
        carry_ref[...] = jnp.broadcast_to(carry, carry_ref.shape)
        acc_ref[...] = acc
        alive_ref[0] = (jnp.max(carry) >= SB_DEAD).astype(jnp.int32)

    carry = jnp.zeros((SB_HEADS, 1), F32)
    acc = jnp.zeros((SB_HEADS, LANES), F32)
    for blk in range(SB_HEAD_PAGES // SB_BLK - 1, -1, -1):
        pages = range(blk * SB_BLK, (blk + 1) * SB_BLK)
        carry, acc = block(jnp.concatenate([k_refs[i][...] for i in pages], axis=1),
                           jnp.concatenate([v_refs[i][...] for i in pages], axis=1), carry, acc)
    publish(carry, acc)

    @pl.when(alive_ref[0] == 1)
    def _():
        def older(i, _):
            @pl.when(alive_ref[0] == 1)
            def _():
                first = n_pages - SB_HEAD_PAGES - SB_BLK * (i + 1)
                copies = []
                for j in range(SB_BLK):
                    pid = pt_ref[b * n_pages + first + j]
                    copies.append(pltpu.make_async_copy(kt_hbm.at[layer, pid], kbuf.at[j], sems.at[0, j]))
                    copies.append(pltpu.make_async_copy(vt_hbm.at[layer, pid], vbuf.at[j], sems.at[1, j]))
                for cp in copies:
                    cp.start()
                for cp in copies:
                    cp.wait()
                carry, acc = block(jnp.concatenate([kbuf[j] for j in range(SB_BLK)], axis=1),
                                   jnp.concatenate([vbuf[j] for j in range(SB_BLK)], axis=1),
                                   carry_ref[:, :1], acc_ref[...])
                publish(carry, acc)
            return 0

        lax.fori_loop(0, (n_pages - SB_HEAD_PAGES) // SB_BLK, older, 0)

    acc = acc_ref[...]
    row = lax.broadcasted_iota(jnp.int32, acc.shape, 0)
    o_ref[...] = jnp.where(row < SB_REP, acc, pltpu.roll(acc, 64, 1))[:, :SB_HEAD_DIM]


def sb_sample(sbq, cache_kt, cache_vt, page_flat, wts, layer, n_pages):
    nb = sbq.shape[0]
    page = cache_kt.shape[3]
    head_map = lambda i: (lambda b, pt: (layer, pt[b * n_pages + n_pages - SB_HEAD_PAGES + i], 0, 0))
    head_specs = [pl.BlockSpec((None, None, LANES, page), head_map(i)) for i in range(SB_HEAD_PAGES)]
    grid_spec = pltpu.PrefetchScalarGridSpec(
        num_scalar_prefetch=1,
        grid=(nb,),
        in_specs=[pl.BlockSpec((None, SB_HEADS, LANES), lambda b, pt: (b, 0, 0)),
                  pl.BlockSpec((SB_BLK * page, SB_BLK * page), lambda b, pt: (0, 0)),
                  pl.BlockSpec(memory_space=pl.ANY), pl.BlockSpec(memory_space=pl.ANY)]
        + head_specs * 2,
        out_specs=pl.BlockSpec((None, SB_HEADS, SB_HEAD_DIM), lambda b, pt: (b, 0, 0)),
        scratch_shapes=[pltpu.VMEM((SB_BLK, LANES, page), F32), pltpu.VMEM((SB_BLK, LANES, page), F32),
                        pltpu.SemaphoreType.DMA((2, SB_BLK)),
                        pltpu.VMEM((SB_HEADS, LANES), F32), pltpu.VMEM((SB_HEADS, LANES), F32),
                        pltpu.SMEM((1,), jnp.int32)],
    )
    return pl.pallas_call(
        functools.partial(_sb_dec_body, layer=layer, n_pages=n_pages),
        grid_spec=grid_spec,
        out_shape=jax.ShapeDtypeStruct((nb, SB_HEADS, SB_HEAD_DIM), F32),
        compiler_params=_cp(1),
        name="sb_sample",
    )(page_flat, sbq.reshape(nb, SB_HEADS, LANES), wts['u256'], cache_kt, cache_vt,
      *([cache_kt] * SB_HEAD_PAGES), *([cache_vt] * SB_HEAD_PAGES))


def _strict_upper(n):
    return jnp.asarray(np.tril(np.ones((n, n), np.float32), -1), BF16)


def _prepare_weights(w_in, conv_w, conv_b, conv_ln_g, conv_ln_b, mla_q_norm_g, mla_kv_norm_g, mla_w_uq, mla_w_uk,
                     mla_w_uv, mla_qk_norm_q, mla_qk_norm_k, gmlp_ln_g, gmlp_ln_b, gmlp_ws, gmlp_b, w_branch, w_out,
                     norm_ffn_g, router_group_w, router_group_b, router_expert_w, router_expert_b, moe_w_gate,
                     moe_w_up, moe_w_down, ple_norm_g, ple_w_gate, ple_w_proj, norm_mix_g, past_len):
    depth = w_in.shape[0]
    off, o = {}, 0
    for name, n in IN_SIZES:
        off[name] = (o, o + n)
        o += n
    seg = lambda name: w_in[:, :, off[name][0]:off[name][1]]
    zeros = lambda n: jnp.zeros((depth, D_MODEL, n), w_in.dtype)
    w_small = jnp.concatenate(
        [seg('conv'), seg('gmlp'), seg('mla_q'), seg('sb_q'), seg('mla_kv'), seg('sb_k'), seg('sb_v'),
         zeros(KR_LANE), seg('mla_kr'), zeros(LANES - KR_LANE - MLA_ROPE), zeros(Z_W - Z_KR - LANES)], axis=-1)
    w_gate = seg('gate').reshape(depth, D_MODEL, N_BRANCH, D_MODEL).transpose(0, 2, 1, 3)
    pad_heads = lambda w, d: jnp.pad(w.reshape(depth, w.shape[1], MLA_HEADS, d),
                                     ((0, 0), (0, 0), (0, 0), (0, HEAD_PAD - d))).reshape(depth, w.shape[1], -1)
    place_kr = np.zeros((LANES, MLA_HEADS * HEAD_PAD), np.float32)
    sel_rope = np.zeros((LANES, LANES), np.float32)
    for i in range(MLA_ROPE):
        sel_rope[KR_LANE + i, i] = 1.0
        for h in range(MLA_HEADS):
            place_kr[KR_LANE + i, h * HEAD_PAD + MLA_NOPE + i] = 1.0
    wk = jnp.concatenate([pad_heads(mla_w_uk, MLA_NOPE),
                          jnp.broadcast_to(jnp.asarray(place_kr), (depth,) + place_kr.shape)], axis=1)
    wukt = mla_w_uk.transpose(0, 2, 1)
    wukt_pad = jnp.pad(wukt.reshape(depth, MLA_HEADS, MLA_NOPE, MLA_KV_LORA),
                       ((0, 0), (0, 0), (0, HEAD_PAD - MLA_NOPE), (0, 0)))
    wbd = jnp.einsum('lchd,hg->lhcgd', mla_w_uv.reshape(depth, MLA_KV_LORA, MLA_HEADS, MLA_V),
                     jnp.eye(MLA_HEADS, dtype=mla_w_uv.dtype)).reshape(depth, MLA_HEADS * MLA_KV_LORA, 512)
    place_sb = np.zeros((512, SB_HEADS * LANES), np.float32)
    for h in range(SB_HEADS):
        for d in range(SB_HEAD_DIM):
            place_sb[h * SB_HEAD_DIM + d, h * LANES + (h // SB_REP) * SB_HEAD_DIM + d] = 1.0
    pad_gain = lambda g, s: jnp.pad(g * s, ((0, 0), (0, HEAD_PAD - MLA_QK)))[:, None, :]
    tril = jnp.tril(jnp.ones((CHUNK, CHUNK), bool))
    pc = past_len % CHUNK
    wr = jnp.concatenate([router_expert_w, router_group_w,
                          jnp.zeros((depth, D_MODEL, LANES - N_EXPERTS - N_GROUPS), F32)], axis=-1)
    wr_hi = wr.astype(BF16)
    wr_lo = (wr - wr_hi.astype(F32)).astype(BF16)
    r_bias = jnp.concatenate([router_expert_b, router_group_b,
                              jnp.zeros((depth, LANES - N_EXPERTS - N_GROUPS), F32)], axis=-1)[:, None, :]
    row = lambda v: v[:, None, :]
    return dict(
        w_small=w_small.astype(BF16), w_gate=w_gate.astype(BF16), g_mix=row(norm_mix_g),
        g_qn=row(mla_q_norm_g), g_kvn=row(mla_kv_norm_g),
        wuq=pad_heads(mla_w_uq, MLA_QK).astype(BF16), wk=wk.astype(BF16),
        gq=pad_gain(mla_qk_norm_q, MLA_SCALE), gk=pad_gain(mla_qk_norm_k, 1.0),
        wukt=wukt.astype(BF16), wukt_pad=wukt_pad.astype(BF16), wbd=wbd.astype(BF16),
        sel_rope=jnp.asarray(sel_rope, BF16), place_sb=jnp.asarray(place_sb, BF16),
        g_ln_g=row(gmlp_ln_g), g_ln_b=row(gmlp_ln_b),
        g_wtril=jnp.where(tril, gmlp_ws, 0.0).astype(BF16),
        g_bias=jnp.broadcast_to(gmlp_b[:, :, :, None], gmlp_b.shape + (CHUNK,)),
        g_wdiag=row(jnp.repeat(gmlp_ws[:, :, pc, pc], CHUNK, axis=-1)),
        g_brow=row(jnp.repeat(gmlp_b[:, :, pc], CHUNK, axis=-1)),
        conv_w=conv_w, conv_b=row(conv_b), conv_ln_g=row(conv_ln_g), conv_ln_b=row(conv_ln_b),
        w_branch=w_branch.astype(BF16), w_out=w_out.astype(BF16), g_ffn=row(norm_ffn_g),
        wr_hi=wr_hi, wr_lo=wr_lo, r_bias=r_bias,
        wgu=jnp.concatenate([moe_w_gate, moe_w_up], axis=-1).astype(BF16), wd=moe_w_down.astype(BF16),
        g_ple=row(ple_norm_g), w_ple_gate=ple_w_gate.astype(BF16), w_ple_proj=ple_w_proj.astype(BF16),
        u512=_strict_upper(512), u256=_strict_upper(SB_BLK * CHUNK),
    )


def _rope_tables(pos):
    half = MLA_ROPE // 2
    inv = ROPE_BASE ** (-jnp.arange(half, dtype=F32) / half)
    ang = pos.astype(F32)[:, None] * inv[None, :]
    cos, sin = jnp.cos(ang), jnp.sin(ang)
    n = pos.shape[0]
    z = lambda w: jnp.zeros((n, w), F32)
    c_t = jnp.concatenate([jnp.ones((n, MLA_NOPE), F32), cos, cos, z(HEAD_PAD - MLA_QK)], axis=-1)
    sa_t = jnp.concatenate([z(MLA_NOPE + half), sin, z(HEAD_PAD - MLA_QK)], axis=-1)
    sb_t = jnp.concatenate([z(MLA_NOPE), -sin, z(half + HEAD_PAD - MLA_QK)], axis=-1)
    return c_t, sa_t, sb_t


def _tail(r, xn, branches, p_all, wts, layer, tm_mix, tm_out, tm_moe, tm_ple):
    mixed = gated_mix(xn, branches, wts, layer, tm_mix)
    r1, hn, comb = out_router(r, mixed, wts, layer, tm_out)
    ffn = moe_dense(hn, comb, wts, layer, tm_moe)
    return ple(r1, ffn, p_all, wts, layer, tm_ple)


def kernel(x_prompt, x_sample, cache_mla_ckv, cache_mla_kpe, cache_sb_k, cache_sb_v, state_conv, page_table, p_prompt, p_sample, norm_mix_g, w_in, conv_w, conv_b, conv_ln_g, conv_ln_b, mla_q_norm_g, mla_kv_norm_g, mla_w_uq, mla_w_uk, mla_w_uv, mla_qk_norm_q, mla_qk_norm_k, gmlp_ln_g, gmlp_ln_b, gmlp_ws, gmlp_b, w_branch, w_out, norm_ffn_g, router_group_w, router_group_b, router_expert_w, router_expert_b, moe_w_gate, moe_w_up, moe_w_down, ple_norm_g, ple_w_gate, ple_w_proj):
    n_batch, seq, d = x_prompt.shape
    nb, dec_seq, _ = x_sample.shape
    depth = w_in.shape[0]
    n_pages = page_table.shape[1]
    page = cache_mla_ckv.shape[2]
    past_len = n_pages * page
    assert d == D_MODEL and dec_seq == 1 and page == CHUNK
    assert seq % 512 == 0 and n_pages % MLA_PPS == 0
    assert n_pages >= SB_HEAD_PAGES and (n_pages - SB_HEAD_PAGES) % SB_BLK == 0

    wts = _prepare_weights(w_in, conv_w, conv_b, conv_ln_g, conv_ln_b, mla_q_norm_g, mla_kv_norm_g, mla_w_uq,
                           mla_w_uk, mla_w_uv, mla_qk_norm_q, mla_qk_norm_k, gmlp_ln_g, gmlp_ln_b, gmlp_ws, gmlp_b,
                           w_branch, w_out, norm_ffn_g, router_group_w, router_group_b, router_expert_w,
                           router_expert_b, moe_w_gate, moe_w_up, moe_w_down, ple_norm_g, ple_w_gate, ple_w_proj,
                           norm_mix_g, past_len)
    tabs_p = _rope_tables(jnp.tile(jnp.arange(seq), n_batch))
    tabs_s = _rope_tables(jnp.full((nb,), past_len, jnp.int32))
    page_flat = page_table.reshape(-1).astype(jnp.int32)
    n_pool = cache_sb_k.shape[1]
    cache_kt = cache_sb_k.transpose(0, 1, 3, 4, 2).reshape(depth, n_pool, LANES, page)
    cache_vt = cache_sb_v.transpose(0, 1, 3, 4, 2).reshape(depth, n_pool, LANES, page)
    cache_kpe_t = cache_mla_kpe.transpose(0, 1, 3, 2)
    state_t = state_conv.transpose(0, 2, 1, 3)
    p_p = p_prompt.reshape(depth, n_batch * seq, PLE_DIM)
    p_s = p_sample.reshape(depth, nb, PLE_DIM)

    rp = x_prompt.reshape(n_batch * seq, d)
    rs = x_sample.reshape(nb, d)
    st_p = [[] for _ in range(5)]
    st_s = [[] for _ in range(6)]
    for l in range(depth):
        zp, xnp = rms_matmul(rp, wts['g_mix'], wts['w_small'], l, 512, 768)
        a_p, q_p, k_p, ckv_p, kpe_p, sbq_p, outd_p, _ = branch_prep(zp, tabs_p, wts, l, True, 256)
        outa_p = conv_prompt(a_p, wts, l, n_batch, seq)
        outb_p = mla_prompt(q_p, k_p, ckv_p, wts, l, n_batch, seq)
        outc_p = sb_prompt(sbq_p, zp, wts, n_batch, seq)
        rp = _tail(rp, xnp, (outa_p, outb_p, outc_p, outd_p), p_p, wts, l, 1024, 256, 1024, 512)
        st_p[0].append(a_p.reshape(n_batch, seq, 512)[:, seq - (CONV_WIDTH - 1):])
        st_p[1].append(ckv_p.reshape(n_batch, seq, MLA_KV_LORA))
        st_p[2].append(kpe_p[:, KR_LANE:KR_LANE + MLA_ROPE].reshape(n_batch, seq, MLA_ROPE))
        st_p[3].append(zp[:, Z_SK:Z_SK + 128].reshape(n_batch, seq, SB_KV_HEADS, SB_HEAD_DIM))
        st_p[4].append(zp[:, Z_SV:Z_SV + 128].reshape(n_batch, seq, SB_KV_HEADS, SB_HEAD_DIM))
        zs, xns = rms_matmul(rs, wts['g_mix'], wts['w_small'], l, nb, 768)
        a_s, q_s, k_s, ckv_s, kpe_s, sbq_s, outd_s, vln_s = branch_prep(zs, tabs_s, wts, l, False, nb)
        outa_s = conv_sample(state_t, a_s, wts, l)
        qabs, qrope = mla_sample_prep(q_s, wts, l)
        o_lat = mla_sample(q_s, k_s, qabs, qrope, ckv_s, cache_mla_ckv, cache_kpe_t, page_flat, wts, l, n_pages)
        outb_s = latent_value(o_lat.reshape(nb, MLA_HEADS * MLA_KV_LORA), wts, l)
        outc_s = sb_sample(sbq_s, cache_kt, cache_vt, page_flat, wts, l, n_pages)
        outc_s = outc_s.reshape(nb, 512).astype(BF16)
        rs = _tail(rs, xns, (outa_s, outb_s, outc_s, outd_s), p_s, wts, l, nb, nb, nb, nb)
        st_s[0].append(jnp.concatenate([state_conv[l][:, 1:], a_s[:, None, :]], axis=1))
        st_s[1].append(ckv_s.reshape(nb, 1, MLA_KV_LORA))
        st_s[2].append(kpe_s[:, KR_LANE:KR_LANE + MLA_ROPE].reshape(nb, 1, MLA_ROPE))
        st_s[3].append(zs[:, Z_SK:Z_SK + 128].reshape(nb, 1, SB_KV_HEADS, SB_HEAD_DIM))
        st_s[4].append(zs[:, Z_SV:Z_SV + 128].reshape(nb, 1, SB_KV_HEADS, SB_HEAD_DIM))
        st_s[5].append(vln_s.reshape(nb, 1, 512))
    conv_state_prompt, mla_ckv_prompt, mla_kpe_prompt, sb_k_prompt, sb_v_prompt = [jnp.stack(a) for a in st_p]
    conv_state_sample, mla_ckv_sample, mla_kpe_sample, sb_k_sample, sb_v_sample, gmlp_v_sample = [
        jnp.stack(a) for a in st_s]
    return (rp.reshape(n_batch, seq, d), rs.reshape(nb, 1, d), conv_state_prompt, conv_state_sample,
            mla_ckv_prompt, mla_ckv_sample, mla_kpe_prompt, mla_kpe_sample, sb_k_prompt, sb_k_sample,
            sb_v_prompt, sb_v_sample, gmlp_v_sample)
```

```python
import functools
import math

import numpy as np
import jax
import jax.numpy as jnp
from jax import lax
from jax.experimental import pallas as pl
from jax.experimental.pallas import tpu as pltpu

F32 = jnp.float32
BF16 = jnp.bfloat16

D_MODEL = 2048
N_BRANCH = 4
MIX_W = 512
PLE_DIM = 256
EPS = 1e-6
CONV_WIDTH = 31
MLA_HEADS = 8
MLA_NOPE = 64
MLA_ROPE = 32
MLA_QK = MLA_NOPE + MLA_ROPE
MLA_V = 64
MLA_Q_LORA = 512
MLA_KV_LORA = 256
MLA_SCALE = MLA_QK ** -0.5
ROPE_BASE = 10000.0
SB_HEADS = 8
SB_KV_HEADS = 2
SB_REP = SB_HEADS // SB_KV_HEADS
SB_HEAD_DIM = 64
SB_SCALE = SB_HEAD_DIM ** -0.5
GMLP_GROUPS = 4
CHUNK = 128
N_GROUPS = 4
EXPERTS_PER_GROUP = 8
N_EXPERTS = N_GROUPS * EXPERTS_PER_GROUP
D_EXPERT = 256

LANES = 128
HEAD_PAD = LANES
NEG_BIG = -1e30
SB_DEAD = -104.0

Z_CONV, Z_GMLP, Z_MQ, Z_SQ, Z_MKV, Z_SK, Z_SV, Z_KR = 0, 1024, 2048, 2560, 3072, 3328, 3456, 3584
Z_W = 3840
KR_LANE = 64

IN_SIZES = (('conv', 1024), ('mla_q', 512), ('mla_kv', 256), ('mla_kr', 32), ('sb_q', 512), ('sb_k', 128),
            ('sb_v', 128), ('gmlp', 1024), ('gate', N_BRANCH * D_MODEL))


def _cp(n_axes, vmem_mb=None):
    return pltpu.CompilerParams(
        dimension_semantics=("arbitrary",) * n_axes,
        vmem_limit_bytes=None if vmem_mb is None else vmem_mb * 2 ** 20)


def _dot(a, b):
    return jnp.dot(a, b, preferred_element_type=F32)


def _dot_nt(a, b):
    return lax.dot_general(a, b, (((1,), (1,)), ((), ())), preferred_element_type=F32)


def _rms_rows(x, g):
    return x * lax.rsqrt(jnp.mean(x * x, axis=-1, keepdims=True) + EPS) * g


def _ln_rows(x, g, b):
    xc = x - jnp.mean(x, axis=-1, keepdims=True)
    return xc * lax.rsqrt(jnp.mean(xc * xc, axis=-1, keepdims=True) + EPS) * g + b


def _logsig(z):
    return jnp.minimum(z, 0.0) - jnp.log1p(jnp.exp(-jnp.abs(z)))


def _split_bf16(x):
    hi = x.astype(BF16)
    lo = (x - hi.astype(F32)).astype(BF16)
    return hi, lo


def _rms_matmul_body(x_ref, g_ref, w_ref, z_ref, xn_ref):
    @pl.when(pl.program_id(1) == 0)
    def _():
        xn_ref[...] = _rms_rows(x_ref[...], g_ref[...]).astype(BF16)

    z_ref[...] = _dot(xn_ref[...], w_ref[...])


def rms_matmul(x, g, w_all, layer, tm, tn):
    m, k = x.shape
    n = w_all.shape[-1]
    return pl.pallas_call(
        _rms_matmul_body,
        grid=(m // tm, n // tn),
        in_specs=[pl.BlockSpec((tm, k), lambda i, j: (i, 0)),
                  pl.BlockSpec((None, 1, k), lambda i, j: (layer, 0, 0)),
                  pl.BlockSpec((None, k, tn), lambda i, j: (layer, 0, j))],
        out_specs=[pl.BlockSpec((tm, tn), lambda i, j: (i, j)),
                   pl.BlockSpec((tm, k), lambda i, j: (i, 0))],
        out_shape=[jax.ShapeDtypeStruct((m, n), F32), jax.ShapeDtypeStruct((m, k), BF16)],
        compiler_params=_cp(2, 48),
        name="rms_matmul",
    )(x, g, w_all)


def _rope_slab(x, c, sa, sb):
    return x * c + pltpu.roll(x, 16, 1) * sa + pltpu.roll(x, LANES - 16, 1) * sb


def _prep_body(z_ref, c_ref, sa_ref, sb_ref, gqn_ref, gkvn_ref, wuq_ref, wk_ref, gq_ref, gk_ref, place_ref,
               glg_ref, glb_ref, gw_ref, gb_ref,
               a_ref, q_ref, k_ref, ckv_ref, kpe_ref, sbq_ref, outd_ref, vln_ref, *, is_prompt, tm):
    c, sa, sb = c_ref[...], sa_ref[...], sb_ref[...]
    a_ref[...] = z_ref[:, Z_CONV:Z_CONV + 512] * jax.nn.sigmoid(z_ref[:, Z_CONV + 512:Z_CONV + 1024])
    cq = _rms_rows(z_ref[:, Z_MQ:Z_MQ + 512], gqn_ref[...]).astype(BF16)
    ckv = _rms_rows(z_ref[:, Z_MKV:Z_MKV + 256], gkvn_ref[...])
    ckv_ref[...] = ckv
    kpe = _rope_slab(z_ref[:, Z_KR:Z_KR + LANES], c, sa, sb)
    kpe_ref[...] = kpe
    q0 = _dot(cq, wuq_ref[...])
    kcat = jnp.concatenate([ckv, kpe], axis=-1).astype(BF16)
    k0 = _dot(kcat, wk_ref[...])
    gq, gk = gq_ref[...], gk_ref[...]
    for h in range(MLA_HEADS):
        sl = slice(h * HEAD_PAD, (h + 1) * HEAD_PAD)
        qh = _rope_slab(q0[:, sl], c, sa, sb)
        qh = qh * lax.rsqrt(jnp.sum(qh * qh, -1, keepdims=True) * (1.0 / MLA_QK) + EPS) * gq
        q_ref[:, sl] = qh.astype(BF16)
        kh = k0[:, sl]
        kh = kh * lax.rsqrt(jnp.sum(kh * kh, -1, keepdims=True) * (1.0 / MLA_QK) + EPS) * gk
        k_ref[:, sl] = kh.astype(BF16)
    sq = (z_ref[:, Z_SQ:Z_SQ + 512] * SB_SCALE).astype(BF16)
    sbq_ref[...] = _dot(sq, place_ref[...]).astype(BF16)
    ge = jax.nn.gelu(z_ref[:, Z_GMLP:Z_GMLP + 1024])
    u = ge[:, :512]
    vln = _ln_rows(ge[:, 512:], glg_ref[...], glb_ref[...])
    vln_ref[...] = vln
    if is_prompt:
        vb = vln.astype(BF16)
        for ch in range(tm // CHUNK):
            rs = slice(ch * CHUNK, (ch + 1) * CHUNK)
            for g in range(GMLP_GROUPS):
                cs = slice(g * 128, (g + 1) * 128)
                mix = _dot(gw_ref[g], vb[rs, cs]) + gb_ref[g]
                outd_ref[rs, cs] = (u[rs, cs] * mix).astype(BF16)
    else:
        outd_ref[...] = (u * (vln * gw_ref[...] + gb_ref[...])).astype(BF16)


def branch_prep(z, tabs, wts, layer, is_prompt, tm):
    m = z.shape[0]
    c_t, sa_t, sb_t = tabs
    row = lambda i: (i, 0)
    lay3 = lambda i: (layer, 0, 0)
    if is_prompt:
        gw, gb = wts['g_wtril'], wts['g_bias']
        gw_spec = pl.BlockSpec((None, GMLP_GROUPS, CHUNK, CHUNK), lambda i: (layer, 0, 0, 0))
        gb_spec = pl.BlockSpec((None, GMLP_GROUPS, CHUNK, CHUNK), lambda i: (layer, 0, 0, 0))
    else:
        gw, gb = wts['g_wdiag'], wts['g_brow']
        gw_spec = pl.BlockSpec((None, 1, 512), lay3)
        gb_spec = pl.BlockSpec((None, 1, 512), lay3)
    in_specs = [
        pl.BlockSpec((tm, Z_W), row),
        pl.BlockSpec((tm, LANES), row), pl.BlockSpec((tm, LANES), row), pl.BlockSpec((tm, LANES), row),
        pl.BlockSpec((None, 1, 512), lay3), pl.BlockSpec((None, 1, 256), lay3),
        pl.BlockSpec((None, 512, 1024), lay3), pl.BlockSpec((None, 384, 1024), lay3),
        pl.BlockSpec((None, 1, LANES), lay3), pl.BlockSpec((None, 1, LANES), lay3),
        pl.BlockSpec((512, 1024), lambda i: (0, 0)),
        pl.BlockSpec((None, 1, 512), lay3), pl.BlockSpec((None, 1, 512), lay3),
        gw_spec, gb_spec,
    ]
    outs = [(512, F32), (1024, BF16), (1024, BF16), (256, F32), (LANES, F32), (1024, BF16), (512, BF16), (512, F32)]
    return pl.pallas_call(
        functools.partial(_prep_body, is_prompt=is_prompt, tm=tm),
        grid=(m // tm,),
        in_specs=in_specs,
        out_specs=[pl.BlockSpec((tm, w), row) for w, _ in outs],
        out_shape=[jax.ShapeDtypeStruct((m, w), dt) for w, dt in outs],
        compiler_params=_cp(1, 48),
        name="branch_prep_p" if is_prompt else "branch_prep_s",
    )(z, c_t, sa_t, sb_t, wts['g_qn'], wts['g_kvn'], wts['wuq'], wts['wk'], wts['gq'], wts['gk'], wts['place_sb'],
      wts['g_ln_g'], wts['g_ln_b'], gw, gb)


CONV_HALO = 32


def _conv_p_body(a_ref, w_ref, b_ref, g_ref, bb_ref, o_ref, buf_ref, y_ref, *, tl):
    @pl.when(pl.program_id(1) == 0)
    def _():
        buf_ref[0:CONV_HALO, :] = jnp.zeros((CONV_HALO, 512), F32)

    buf_ref[CONV_HALO:CONV_HALO + tl, :] = a_ref[...]
    rb = 128
    shift = CONV_HALO - (CONV_WIDTH - 1)
    for r in range(tl // rb):
        for cb in range(4):
            cs = slice(cb * LANES, (cb + 1) * LANES)
            acc = jnp.zeros((rb, LANES), F32)
            for j in range(CONV_WIDTH):
                lo = r * rb + j + shift
                acc = acc + w_ref[j:j + 1, cs] * buf_ref[lo:lo + rb, cs]
            y_ref[r * rb:(r + 1) * rb, cs] = acc
    y = _ln_rows(y_ref[...] + b_ref[...], g_ref[...], bb_ref[...])
    o_ref[...] = jax.nn.silu(y).astype(BF16)
    buf_ref[0:CONV_HALO, :] = buf_ref[tl:tl + CONV_HALO, :]


def conv_prompt(a, wts, layer, n_batch, seq, tl=256):
    nt = seq // tl
    lay3 = lambda b, i: (layer, 0, 0)
    return pl.pallas_call(
        functools.partial(_conv_p_body, tl=tl),
        grid=(n_batch, nt),
        in_specs=[pl.BlockSpec((tl, 512), lambda b, i: (b * nt + i, 0)),
                  pl.BlockSpec((None, CONV_WIDTH, 512), lay3),
                  pl.BlockSpec((None, 1, 512), lay3), pl.BlockSpec((None, 1, 512), lay3),
                  pl.BlockSpec((None, 1, 512), lay3)],
        out_specs=pl.BlockSpec((tl, 512), lambda b, i: (b * nt + i, 0)),
        out_shape=jax.ShapeDtypeStruct((n_batch * seq, 512), BF16),
        scratch_shapes=[pltpu.VMEM((tl + CONV_HALO, 512), F32), pltpu.VMEM((tl, 512), F32)],
        compiler_params=_cp(2),
        name="conv_prompt",
    )(a, wts['conv_w'], wts['conv_b'], wts['conv_ln_g'], wts['conv_ln_b'])


def _conv_s_body(st_ref, a_ref, w_ref, b_ref, g_ref, bb_ref, o_ref):
    y = a_ref[...] * w_ref[CONV_WIDTH - 1:CONV_WIDTH, :] + b_ref[...]
    for j in range(CONV_WIDTH - 1):
        y = y + st_ref[j] * w_ref[j:j + 1, :]
    o_ref[...] = jax.nn.silu(_ln_rows(y, g_ref[...], bb_ref[...])).astype(BF16)


def conv_sample(state_t, a, wts, layer):
    nb = a.shape[0]
    lay3 = lambda i: (layer, 0, 0)
    return pl.pallas_call(
        _conv_s_body,
        grid=(1,),
        in_specs=[pl.BlockSpec((None, CONV_WIDTH - 1, nb, 512), lambda i: (layer, 0, 0, 0)),
                  pl.BlockSpec((nb, 512), lambda i: (0, 0)),
                  pl.BlockSpec((None, CONV_WIDTH, 512), lay3),
                  pl.BlockSpec((None, 1, 512), lay3), pl.BlockSpec((None, 1, 512), lay3),
                  pl.BlockSpec((None, 1, 512), lay3)],
        out_specs=pl.BlockSpec((nb, 512), lambda i: (0, 0)),
        out_shape=jax.ShapeDtypeStruct((nb, 512), BF16),
        compiler_params=_cp(1, 48),
        name="conv_sample",
    )(state_t, a, wts['conv_w'], wts['conv_b'], wts['conv_ln_g'], wts['conv_ln_b'])


def _pair_tables(n_tiles, descending):
    qs, ks = [], []
    for qi in range(n_tiles):
        kis = range(qi, -1, -1) if descending else range(qi + 1)
        for ki in kis:
            qs.append(qi)
            ks.append(ki)
    return np.asarray(qs, np.int32), np.asarray(ks, np.int32)


def _mla_flash_body(qt_ref, kt_ref, q_ref, k_ref, c_ref, wbd_ref, o_ref, m_ref, l_ref, acc_ref, *, tq, tk):
    s_idx = pl.program_id(1)
    qi, ki = qt_ref[s_idx], kt_ref[s_idx]

    @pl.when(ki == 0)
    def _():
        m_ref[...] = jnp.full(m_ref.shape, NEG_BIG, F32)
        l_ref[...] = jnp.zeros(l_ref.shape, F32)
        acc_ref[...] = jnp.zeros(acc_ref.shape, F32)

    def step(masked):
        c = c_ref[...].astype(BF16)
        if masked:
            allow = (lax.broadcasted_iota(jnp.int32, (tq, tk), 1) <= lax.broadcasted_iota(jnp.int32, (tq, tk), 0))
        for h in range(MLA_HEADS):
            sl = slice(h * HEAD_PAD, (h + 1) * HEAD_PAD)
            s = _dot_nt(q_ref[:, sl], k_ref[:, sl])
            if masked:
                s = jnp.where(allow, s, NEG_BIG)
            m_prev = m_ref[h]
            m_new = jnp.maximum(m_prev, jnp.max(s, -1, keepdims=True))
            alpha = jnp.exp(m_prev - m_new)
            p = jnp.exp(s - jnp.tile(m_new, (1, tk // LANES)))
            l_ref[h] = alpha * l_ref[h] + jnp.sum(p, -1, keepdims=True)
            acc_ref[h] = jnp.tile(alpha, (1, MLA_KV_LORA // LANES)) * acc_ref[h] + _dot(p.astype(BF16), c)
            m_ref[h] = m_new

    @pl.when(ki < qi)
    def _():
        step(False)

    @pl.when(ki == qi)
    def _():
        step(True)
        o = jnp.concatenate(
            [(acc_ref[h] * jnp.tile(1.0 / l_ref[h], (1, MLA_KV_LORA // LANES))).astype(BF16)
             for h in range(MLA_HEADS)], axis=-1)
        o_ref[...] = _dot(o, wbd_ref[...]).astype(BF16)


def mla_prompt(q, k, ckv, wts, layer, n_batch, seq, tq=512):
    nt = seq // tq
    qs, ks = _pair_tables(nt, descending=False)
    qmap = lambda b, s, qt, kt: (b * nt + qt[s], 0)
    kmap = lambda b, s, qt, kt: (b * nt + kt[s], 0)
    grid_spec = pltpu.PrefetchScalarGridSpec(
        num_scalar_prefetch=2,
        grid=(n_batch, len(qs)),
        in_specs=[pl.BlockSpec((tq, 1024), qmap), pl.BlockSpec((tq, 1024), kmap),
                  pl.BlockSpec((tq, MLA_KV_LORA), kmap),
                  pl.BlockSpec((None, MLA_HEADS * MLA_KV_LORA, 512), lambda b, s, qt, kt: (layer, 0, 0))],
        out_specs=pl.BlockSpec((tq, 512), qmap),
        scratch_shapes=[pltpu.VMEM((MLA_HEADS, tq, LANES), F32), pltpu.VMEM((MLA_HEADS, tq, LANES), F32),
                        pltpu.VMEM((MLA_HEADS, tq, MLA_KV_LORA), F32)],
    )
    return pl.pallas_call(
        functools.partial(_mla_flash_body, tq=tq, tk=tq),
        grid_spec=grid_spec,
        out_shape=jax.ShapeDtypeStruct((n_batch * seq, 512), BF16),
        compiler_params=_cp(2, 48),
        name="mla_prompt",
    )(jnp.asarray(qs), jnp.asarray(ks), q, k, ckv, wts['wbd'])


def _sb_tile(z, carry, u_mat, v_bf, allow, v_transposed=False):
    lb = _logsig(z)
    lk = lb - z
    if allow is not None:
        lk = jnp.where(allow, lk, 0.0)
    hi, lo = _split_bf16(lk)
    later = jnp.tile(carry, (1, z.shape[1] // LANES)) + _dot(hi, u_mat) + _dot(lo, u_mat)
    a = jnp.exp(lb + later)
    if allow is not None:
        a = jnp.where(allow, a, 0.0)
    pv = _dot_nt(a.astype(BF16), v_bf) if v_transposed else _dot(a.astype(BF16), v_bf)
    return pv, carry + jnp.sum(lk, -1, keepdims=True)


def _pick_group_half(acc_lo, acc_hi, group):
    lane = lax.broadcasted_iota(jnp.int32, acc_lo.shape, 1)
    if group == 0:
        return jnp.where(lane < 64, acc_lo, pltpu.roll(acc_hi, 64, 1))
    return jnp.where(lane < 64, pltpu.roll(acc_lo, 64, 1), acc_hi)


def _sb_flash_body(qt_ref, kt_ref, q_ref, k_ref, v_ref, u_ref, o_ref, carry_ref, acc_ref, alive_ref, *, tq, tk):
    s_idx = pl.program_id(1)
    qi, ki = qt_ref[s_idx], kt_ref[s_idx]

    @pl.when(ki == qi)
    def _():
        carry_ref[...] = jnp.zeros(carry_ref.shape, F32)
        acc_ref[...] = jnp.zeros(acc_ref.shape, F32)
        alive_ref[0] = 1

    def step(masked):
        k_bf = k_ref[...].astype(BF16)
        v_bf = v_ref[...].astype(BF16)
        u_mat = u_ref[...]
        allow = None
        if masked:
            allow = (lax.broadcasted_iota(jnp.int32, (tq, tk), 1) < lax.broadcasted_iota(jnp.int32, (tq, tk), 0))
        worst = None
        for h in range(SB_HEADS):
            z = _dot_nt(q_ref[:, h * LANES:(h + 1) * LANES], k_bf)
            pv, carry = _sb_tile(z, carry_ref[h], u_mat, v_bf, allow)
            acc_ref[h] = acc_ref[h] + pv
            carry_ref[h] = carry
            hmax = jnp.max(carry)
            worst = hmax if worst is None else jnp.maximum(worst, hmax)
        alive_ref[0] = (worst >= SB_DEAD).astype(jnp.int32)

    @pl.when(jnp.logical_and(ki == qi, alive_ref[0] == 1))
    def _():
        step(True)

    @pl.when(jnp.logical_and(ki < qi, alive_ref[0] == 1))
    def _():
        step(False)

    @pl.when(ki == 0)
    def _():
        for m in range(SB_HEADS // 2):
            blk = _pick_group_half(acc_ref[2 * m], acc_ref[2 * m + 1], (2 * m) // SB_REP)
            o_ref[:, m * LANES:(m + 1) * LANES] = blk.astype(BF16)


def sb_prompt(sbq, z, wts, n_batch, seq, tq=256):
    nt = seq // tq
    qs, ks = _pair_tables(nt, descending=True)
    kcol, vcol = Z_SK // LANES, Z_SV // LANES
    grid_spec = pltpu.PrefetchScalarGridSpec(
        num_scalar_prefetch=2,
        grid=(n_batch, len(qs)),
        in_specs=[pl.BlockSpec((tq, 1024), lambda b, s, qt, kt: (b * nt + qt[s], 0)),
                  pl.BlockSpec((tq, LANES), lambda b, s, qt, kt: (b * nt + kt[s], kcol)),
                  pl.BlockSpec((tq, LANES), lambda b, s, qt, kt: (b * nt + kt[s], vcol)),
                  pl.BlockSpec((tq, tq), lambda b, s, qt, kt: (0, 0))],
        out_specs=pl.BlockSpec((tq, 512), lambda b, s, qt, kt: (b * nt + qt[s], 0)),
        scratch_shapes=[pltpu.VMEM((SB_HEADS, tq, LANES), F32), pltpu.VMEM((SB_HEADS, tq, LANES), F32),
                        pltpu.SMEM((1,), jnp.int32)],
    )
    return pl.pallas_call(
        functools.partial(_sb_flash_body, tq=tq, tk=tq),
        grid_spec=grid_spec,
        out_shape=jax.ShapeDtypeStruct((n_batch * seq, 512), BF16),
        compiler_params=_cp(2, 48),
        name="sb_prompt",
    )(jnp.asarray(qs), jnp.asarray(ks), sbq, z, z, _strict_upper(tq))


def _mix_body(xn_ref, a_ref, b_ref, c_ref, d_ref, wg_ref, wb_ref, o_ref):
    xn = xn_ref[...]
    acc = None
    for i, br in enumerate((a_ref, b_ref, c_ref, d_ref)):
        term = jax.nn.sigmoid(_dot(xn, wg_ref[i])) * _dot(br[...], wb_ref[i])
        acc = term if acc is None else acc + term
    o_ref[...] = acc.astype(BF16)


def gated_mix(xn, branches, wts, layer, tm, tn=256):
    m = xn.shape[0]
    row = lambda i, j: (i, 0)
    return pl.pallas_call(
        _mix_body,
        grid=(m // tm, D_MODEL // tn),
        in_specs=[pl.BlockSpec((tm, D_MODEL), row)] + [pl.BlockSpec((tm, MIX_W), row)] * 4 + [
            pl.BlockSpec((None, N_BRANCH, D_MODEL, tn), lambda i, j: (layer, 0, 0, j)),
            pl.BlockSpec((None, N_BRANCH, MIX_W, tn), lambda i, j: (layer, 0, 0, j))],
        out_specs=pl.BlockSpec((tm, tn), lambda i, j: (i, j)),
        out_shape=jax.ShapeDtypeStruct((m, D_MODEL), BF16),
        compiler_params=_cp(2, 48),
        name="gated_mix",
    )(xn, *branches, wts['w_gate'], wts['w_branch'])


def _route(logits):
    lane = lax.broadcasted_iota(jnp.int32, logits.shape, 1)
    lane_f = lane.astype(F32)
    far = float(LANES)
    gmask = (lane >> 2) == (N_EXPERTS // 4)
    gl = jnp.where(gmask, logits, NEG_BIG)
    gmax = jnp.max(gl, -1, keepdims=True)
    g_p = 1.0 / jnp.sum(jnp.where(gmask, jnp.exp(gl - gmax), 0.0), -1, keepdims=True)
    gidx = jnp.min(jnp.where(gl == gmax, lane_f, far), -1, keepdims=True) - float(N_EXPERTS)
    el = jnp.where((lane >> 3).astype(F32) == gidx, logits, NEG_BIG)
    m1 = jnp.max(el, -1, keepdims=True)
    i1 = jnp.min(jnp.where(el == m1, lane_f, far), -1, keepdims=True)
    el2 = jnp.where(lane_f == i1, NEG_BIG, el)
    m2 = jnp.max(el2, -1, keepdims=True)
    i2 = jnp.min(jnp.where(el2 == m2, lane_f, far), -1, keepdims=True)
    e2 = jnp.exp(m2 - m1)
    inv = 1.0 / (1.0 + e2)
    return jnp.where(lane_f == i1, inv * g_p, 0.0) + jnp.where(lane_f == i2, e2 * inv * g_p, 0.0)


def _out_router_body(r_ref, mix_ref, wo_ref, g_ref, whi_ref, wlo_ref, rb_ref, r1_ref, hn_ref, comb_ref):
    r1 = r_ref[...] + _dot(mix_ref[...], wo_ref[...])
    r1_ref[...] = r1
    h = _rms_rows(r1, g_ref[...])
    hi, lo = _split_bf16(h)
    hn_ref[...] = hi
    logits = _dot(hi, whi_ref[...]) + _dot(lo, whi_ref[...]) + _dot(hi, wlo_ref[...]) + rb_ref[...]
    comb_ref[...] = _route(logits)


def out_router(r, mixed, wts, layer, tm):
    m = r.shape[0]
    row = lambda i: (i, 0)
    lay3 = lambda i: (layer, 0, 0)
    return pl.pallas_call(
        _out_router_body,
        grid=(m // tm,),
        in_specs=[pl.BlockSpec((tm, D_MODEL), row), pl.BlockSpec((tm, D_MODEL), row),
                  pl.BlockSpec((None, D_MODEL, D_MODEL), lay3), pl.BlockSpec((None, 1, D_MODEL), lay3),
                  pl.BlockSpec((None, D_MODEL, LANES), lay3), pl.BlockSpec((None, D_MODEL, LANES), lay3),
                  pl.BlockSpec((None, 1, LANES), lay3)],
        out_specs=[pl.BlockSpec((tm, D_MODEL), row), pl.BlockSpec((tm, D_MODEL), row),
                   pl.BlockSpec((tm, LANES), row)],
        out_shape=[jax.ShapeDtypeStruct((m, D_MODEL), F32), jax.ShapeDtypeStruct((m, D_MODEL), BF16),
                   jax.ShapeDtypeStruct((m, LANES), F32)],
        compiler_params=_cp(1, 48),
        name="out_router",
    )(r, mixed, wts['w_out'], wts['g_ffn'], wts['wr_hi'], wts['wr_lo'], wts['r_bias'])


def _moe_body(hn_ref, comb_ref, wgu_ref, wd_ref, o_ref):
    e = pl.program_id(1)

    @pl.when(e == 0)
    def _():
        o_ref[...] = jnp.zeros(o_ref.shape, F32)

    gu = _dot(hn_ref[...], wgu_ref[...])
    comb = comb_ref[...]
    lane = lax.broadcasted_iota(jnp.int32, comb.shape, 1)
    ce = jnp.sum(jnp.where(lane == e, comb, 0.0), -1, keepdims=True)
    hid = jax.nn.silu(gu[:, :D_EXPERT]) * gu[:, D_EXPERT:] * ce
    o_ref[...] += _dot(hid.astype(BF16), wd_ref[...])


def moe_dense(hn, comb, wts, layer, tm):
    m = hn.shape[0]
    row = lambda i, e: (i, 0)
    return pl.pallas_call(
        _moe_body,
        grid=(m // tm, N_EXPERTS),
        in_specs=[pl.BlockSpec((tm, D_MODEL), row), pl.BlockSpec((tm, LANES), row),
                  pl.BlockSpec((None, None, D_MODEL, 2 * D_EXPERT), lambda i, e: (layer, e, 0, 0)),
                  pl.BlockSpec((None, None, D_EXPERT, D_MODEL), lambda i, e: (layer, e, 0, 0))],
        out_specs=pl.BlockSpec((tm, D_MODEL), row),
        out_shape=jax.ShapeDtypeStruct((m, D_MODEL), F32),
        compiler_params=_cp(2, 48),
        name="moe_dense",
    )(hn, comb, wts['wgu'], wts['wd'])


def _ple_body(r_ref, f_ref, p_ref, g_ref, wg_ref, wp_ref, o_ref):
    r2 = r_ref[...] + f_ref[...]
    gate = jax.nn.sigmoid(_dot(_rms_rows(r2, g_ref[...]).astype(BF16), wg_ref[...]))
    o_ref[...] = r2 + gate * _dot(p_ref[...].astype(BF16), wp_ref[...])


def ple(r, ffn, p_all, wts, layer, tm):
    m = r.shape[0]
    row = lambda i: (i, 0)
    lay3 = lambda i: (layer, 0, 0)
    return pl.pallas_call(
        _ple_body,
        grid=(m // tm,),
        in_specs=[pl.BlockSpec((tm, D_MODEL), row), pl.BlockSpec((tm, D_MODEL), row),
                  pl.BlockSpec((None, tm, PLE_DIM), lambda i: (layer, i, 0)),
                  pl.BlockSpec((None, 1, D_MODEL), lay3),
                  pl.BlockSpec((None, D_MODEL, D_MODEL), lay3),
                  pl.BlockSpec((None, PLE_DIM, D_MODEL), lay3)],
        out_specs=pl.BlockSpec((tm, D_MODEL), row),
        out_shape=jax.ShapeDtypeStruct((m, D_MODEL), F32),
        compiler_params=_cp(1, 48),
        name="ple",
    )(r, ffn, p_all, wts['g_ple'], wts['w_ple_gate'], wts['w_ple_proj'])


def _mla_sprep_body(q_ref, gk_ref, wukt_ref, sel_ref, qabs_ref, qrope_ref):
    gk = gk_ref[...]
    for h in range(MLA_HEADS):
        sl = slice(h * HEAD_PAD, (h + 1) * HEAD_PAD)
        qg = (q_ref[:, sl].astype(F32) * gk).astype(BF16)
        qabs_ref[:, h * MLA_KV_LORA:(h + 1) * MLA_KV_LORA] = _dot(qg, wukt_ref[h]).astype(BF16)
        qrope_ref[:, sl] = _dot(qg, sel_ref[...]).astype(BF16)


def mla_sample_prep(q, wts, layer):
    nb = q.shape[0]
    return pl.pallas_call(
        _mla_sprep_body,
        grid=(1,),
        in_specs=[pl.BlockSpec((nb, 1024), lambda i: (0, 0)),
                  pl.BlockSpec((None, 1, LANES), lambda i: (layer, 0, 0)),
                  pl.BlockSpec((None, MLA_HEADS, HEAD_PAD, MLA_KV_LORA), lambda i: (layer, 0, 0, 0)),
                  pl.BlockSpec((LANES, LANES), lambda i: (0, 0))],
        out_specs=[pl.BlockSpec((nb, MLA_HEADS * MLA_KV_LORA), lambda i: (0, 0)),
                   pl.BlockSpec((nb, 1024), lambda i: (0, 0))],
        out_shape=[jax.ShapeDtypeStruct((nb, MLA_HEADS * MLA_KV_LORA), BF16),
                   jax.ShapeDtypeStruct((nb, 1024), BF16)],
        compiler_params=_cp(1),
        name="mla_sample_prep",
    )(q, wts['gk'], wts['wukt_pad'], wts['sel_rope'])


MLA_PPS = 32
MLA_SUB = 8


def _mla_dec_body(pt_ref, q8_ref, k8_ref, qabs_ref, qrope_ref, cnew_ref, wukt_ref, ckv_hbm, kpe_hbm, o_ref,
                  ckv_buf, kpe_buf, sems, m_ref, l_ref, acc_ref, *, layer, n_chunks):
    ci = pl.program_id(1)
    step = pl.program_id(0) * n_chunks + ci
    last_step = pl.num_programs(0) * n_chunks - 1
    slot = lax.rem(step, 2)

    def page_copies(s, slot_):
        cps = []
        for i in range(MLA_PPS):
            pid = pt_ref[s * MLA_PPS + i]
            cps.append(pltpu.make_async_copy(ckv_hbm.at[layer, pid], ckv_buf.at[slot_, i], sems.at[0, slot_]))
            cps.append(pltpu.make_async_copy(kpe_hbm.at[layer, pid], kpe_buf.at[slot_, i], sems.at[1, slot_]))
        return cps

    @pl.when(step == 0)
    def _():
        for cp in page_copies(0, 0):
            cp.start()

    @pl.when(ci == 0)
    def _():
        m_ref[...] = jnp.full(m_ref.shape, NEG_BIG, F32)
        l_ref[...] = jnp.zeros(l_ref.shape, F32)
        acc_ref[...] = jnp.zeros(acc_ref.shape, F32)

    for i in range(MLA_PPS):
        pltpu.make_async_copy(ckv_hbm.at[layer, 0], ckv_buf.at[slot, i], sems.at[0, slot]).wait()
        pltpu.make_async_copy(kpe_hbm.at[layer, 0], kpe_buf.at[slot, i], sems.at[1, slot]).wait()

    @pl.when(step < last_step)
    def _():
        for cp in page_copies(jnp.minimum(step + 1, last_step), 1 - slot):
            cp.start()

    qabs = qabs_ref[...]
    qrope = qrope_ref[:, :MLA_ROPE]
    wukt = wukt_ref[...]
    s_parts, c_parts = [], []
    for sub in range(MLA_PPS // MLA_SUB):
        pages = range(sub * MLA_SUB, (sub + 1) * MLA_SUB)
        c = jnp.concatenate([ckv_buf[slot, i] for i in pages], axis=0).astype(BF16)
        kpt = jnp.concatenate([kpe_buf[slot, i] for i in pages], axis=1)
        tk = c.shape[0]
        kt = _dot_nt(wukt, c)
        ssn = jnp.sum((kt * kt).reshape(MLA_HEADS, MLA_NOPE, tk), axis=1)
        ssr = jnp.sum(kpt * kpt, axis=0, keepdims=True)
        raw = _dot_nt(qabs, c) + _dot(qrope, kpt.astype(BF16))
        s_parts.append(raw * lax.rsqrt((ssn + ssr) * (1.0 / MLA_QK) + EPS))
        c_parts.append(c)
    s = jnp.concatenate(s_parts, axis=1)
    c_all = jnp.concatenate(c_parts, axis=0)
    m_prev = m_ref[...]
    m_new = jnp.maximum(m_prev, jnp.max(s, -1, keepdims=True))
    alpha = jnp.exp(m_prev - m_new)
    p = jnp.exp(s - jnp.tile(m_new, (1, s.shape[1] // LANES)))
    l_ref[...] = alpha * l_ref[...] + jnp.sum(p, -1, keepdims=True)
    acc_ref[...] = jnp.tile(alpha, (1, MLA_KV_LORA // LANES)) * acc_ref[...] + _dot(p.astype(BF16), c_all)
    m_ref[...] = m_new

    @pl.when(ci == n_chunks - 1)
    def _():
        s_self = jnp.sum(q8_ref[...].astype(F32) * k8_ref[...].astype(F32), -1, keepdims=True)
        m_prev = m_ref[...]
        m_fin = jnp.maximum(m_prev, s_self)
        alpha = jnp.exp(m_prev - m_fin)
        p_self = jnp.exp(s_self - m_fin[:, :1])
        l_fin = alpha[:, :1] * l_ref[:, :1] + p_self
        acc = alpha[:, :1] * acc_ref[...] + p_self * cnew_ref[...]
        o_ref[...] = acc * (1.0 / l_fin)


def mla_sample(q, k, qabs, qrope, ckv_new, cache_ckv, cache_kpe_t, page_flat, wts, layer, n_pages):
    nb = q.shape[0]
    n_chunks = n_pages // MLA_PPS
    page = cache_ckv.shape[2]
    seq3 = lambda b, c, pt: (b, 0, 0)
    assert n_pages == n_chunks * MLA_PPS

    grid_spec = pltpu.PrefetchScalarGridSpec(
        num_scalar_prefetch=1,
        grid=(nb, n_chunks),
        in_specs=[pl.BlockSpec((None, MLA_HEADS, HEAD_PAD), seq3), pl.BlockSpec((None, MLA_HEADS, HEAD_PAD), seq3),
                  pl.BlockSpec((None, MLA_HEADS, MLA_KV_LORA), seq3), pl.BlockSpec((None, MLA_HEADS, HEAD_PAD), seq3),
                  pl.BlockSpec((None, 1, MLA_KV_LORA), seq3),
                  pl.BlockSpec((None, MLA_HEADS * MLA_NOPE, MLA_KV_LORA), lambda b, c, pt: (layer, 0, 0)),
                  pl.BlockSpec(memory_space=pl.ANY), pl.BlockSpec(memory_space=pl.ANY)],
        out_specs=pl.BlockSpec((None, MLA_HEADS, MLA_KV_LORA), seq3),
        scratch_shapes=[pltpu.VMEM((2, MLA_PPS, page, MLA_KV_LORA), F32), pltpu.VMEM((2, MLA_PPS, MLA_ROPE, page), F32),
                        pltpu.SemaphoreType.DMA((2, 2)),
                        pltpu.VMEM((MLA_HEADS, LANES), F32), pltpu.VMEM((MLA_HEADS, LANES), F32),
                        pltpu.VMEM((MLA_HEADS, MLA_KV_LORA), F32)],
    )
    return pl.pallas_call(
        functools.partial(_mla_dec_body, layer=layer, n_chunks=n_chunks),
        grid_spec=grid_spec,
        out_shape=jax.ShapeDtypeStruct((nb, MLA_HEADS, MLA_KV_LORA), F32),
        compiler_params=_cp(2, 48),
        name="mla_sample",
    )(page_flat, q.reshape(nb, MLA_HEADS, HEAD_PAD), k.reshape(nb, MLA_HEADS, HEAD_PAD),
      qabs.reshape(nb, MLA_HEADS, MLA_KV_LORA), qrope.reshape(nb, MLA_HEADS, HEAD_PAD),
      ckv_new.reshape(nb, 1, MLA_KV_LORA), wts['wukt'], cache_ckv, cache_kpe_t)


def _latent_value_body(o_ref, wbd_ref, out_ref):
    out_ref[...] = _dot(o_ref[...].astype(BF16), wbd_ref[...]).astype(BF16)


def latent_value(o_lat, wts, layer):
    nb = o_lat.shape[0]
    return pl.pallas_call(
        _latent_value_body,
        grid=(1,),
        in_specs=[pl.BlockSpec((nb, MLA_HEADS * MLA_KV_LORA), lambda i: (0, 0)),
                  pl.BlockSpec((None, MLA_HEADS * MLA_KV_LORA, 512), lambda i: (layer, 0, 0))],
        out_specs=pl.BlockSpec((nb, 512), lambda i: (0, 0)),
        out_shape=jax.ShapeDtypeStruct((nb, 512), BF16),
        compiler_params=_cp(1),
        name="latent_value",
    )(o_lat, wts['wbd'])


SB_HEAD_PAGES = 4
SB_BLK = 2


def _sb_dec_body(pt_ref, q_ref, u_ref, kt_hbm, vt_hbm, *rest, layer, n_pages):
    k_refs = rest[:SB_HEAD_PAGES]
    v_refs = rest[SB_HEAD_PAGES:2 * SB_HEAD_PAGES]
    o_ref, kbuf, vbuf, sems, carry_ref, acc_ref, alive_ref = rest[2 * SB_HEAD_PAGES:]
    b = pl.program_id(0)
    q = q_ref[...]
    u_mat = u_ref[...]

    def block(kt, vt, carry, acc):
        pv, carry = _sb_tile(_dot(q, kt.astype(BF16)), carry, u_mat, vt.astype(BF16), None, v_transposed=True)
        return carry, acc + pv

    def publish(carry, acc):
        carry_ref[...] = carry
        acc_ref[...] = acc
        alive_ref[0] = (jnp.max(carry) >= SB_DEAD).astype(jnp.int32)

    carry = jnp.zeros((SB_HEADS, LANES), F32)
    acc = jnp.zeros((SB_HEADS, LANES), F32)
    for blk in range(SB_HEAD_PAGES // SB_BLK - 1, -1, -1):
        pages = range(blk * SB_BLK, (blk + 1) * SB_BLK)
        carry, acc = block(jnp.concatenate([k_refs[i][...] for i in pages], axis=1),
                           jnp.concatenate([v_refs[i][...] for i in pages], axis=1), carry, acc)
    publish(carry, acc)

    @pl.when(alive_ref[0] == 1)
    def _():
        def older(i, _):
            @pl.when(alive_ref[0] == 1)
            def _():
                first = n_pages - SB_HEAD_PAGES - SB_BLK * (i + 1)
                copies = []
                for j in range(SB_BLK):
                    pid = pt_ref[b * n_pages + first + j]
                    copies.append(pltpu.make_async_copy(kt_hbm.at[layer, pid], kbuf.at[j], sems.at[0, j]))
                    copies.append(pltpu.make_async_copy(vt_hbm.at[layer, pid], vbuf.at[j], sems.at[1, j]))
                for cp in copies:
                    cp.start()
                for cp in copies:
                    cp.wait()
                carry, acc = block(jnp.concatenate([kbuf[j] for j in range(SB_BLK)], axis=1),
                                   jnp.concatenate([vbuf[j] for j in range(SB_BLK)], axis=1),
                                   carry_ref[...], acc_ref[...])
                publish(carry, acc)
            return 0

        lax.fori_loop(0, (n_pages - SB_HEAD_PAGES) // SB_BLK, older, 0)

    acc = acc_ref[...]
    row = lax.broadcasted_iota(jnp.int32, acc.shape, 0)
    o_ref[...] = jnp.where(row < SB_REP, acc, pltpu.roll(acc, 64, 1))[:, :SB_HEAD_DIM]


def sb_sample(sbq, cache_kt, cache_vt, page_flat, wts, layer, n_pages):
    nb = sbq.shape[0]
    page = cache_kt.shape[3]
    head_map = lambda i: (lambda b, pt: (layer, pt[b * n_pages + n_pages - SB_HEAD_PAGES + i], 0, 0))
    head_specs = [pl.BlockSpec((None, None, LANES, page), head_map(i)) for i in range(SB_HEAD_PAGES)]
    grid_spec = pltpu.PrefetchScalarGridSpec(
        num_scalar_prefetch=1,
        grid=(nb,),
        in_specs=[pl.BlockSpec((None, SB_HEADS, LANES), lambda b, pt: (b, 0, 0)),
                  pl.BlockSpec((SB_BLK * page, SB_BLK * page), lambda b, pt: (0, 0)),
                  pl.BlockSpec(memory_space=pl.ANY), pl.BlockSpec(memory_space=pl.ANY)]
        + head_specs * 2,
        out_specs=pl.BlockSpec((None, SB_HEADS, SB_HEAD_DIM), lambda b, pt: (b, 0, 0)),
        scratch_shapes=[pltpu.VMEM((SB_BLK, LANES, page), F32), pltpu.VMEM((SB_BLK, LANES, page), F32),
                        pltpu.SemaphoreType.DMA((2, SB_BLK)),
                        pltpu.VMEM((SB_HEADS, LANES), F32), pltpu.VMEM((SB_HEADS, LANES), F32),
                        pltpu.SMEM((1,), jnp.int32)],
    )
    return pl.pallas_call(
        functools.partial(_sb_dec_body, layer=layer, n_pages=n_pages),
        grid_spec=grid_spec,
        out_shape=jax.ShapeDtypeStruct((nb, SB_HEADS, SB_HEAD_DIM), F32),
        compiler_params=_cp(1),
        name="sb_sample",
    )(page_flat, sbq.reshape(nb, SB_HEADS, LANES), wts['u256'], cache_kt, cache_vt,
      *([cache_kt] * SB_HEAD_PAGES), *([cache_vt] * SB_HEAD_PAGES))


def _strict_upper(n):
    return jnp.asarray(np.tril(np.ones((n, n), np.float32), -1), BF16)


def _prepare_weights(w_in, conv_w, conv_b, conv_ln_g, conv_ln_b, mla_q_norm_g, mla_kv_norm_g, mla_w_uq, mla_w_uk,
                     mla_w_uv, mla_qk_norm_q, mla_qk_norm_k, gmlp_ln_g, gmlp_ln_b, gmlp_ws, gmlp_b, w_branch, w_out,
                     norm_ffn_g, router_group_w, router_group_b, router_expert_w, router_expert_b, moe_w_gate,
                     moe_w_up, moe_w_down, ple_norm_g, ple_w_gate, ple_w_proj, norm_mix_g, past_len):
    depth = w_in.shape[0]
    off, o = {}, 0
    for name, n in IN_SIZES:
        off[name] = (o, o + n)
        o += n
    seg = lambda name: w_in[:, :, off[name][0]:off[name][1]]
    zeros = lambda n: jnp.zeros((depth, D_MODEL, n), w_in.dtype)
    w_small = jnp.concatenate(
        [seg('conv'), seg('gmlp'), seg('mla_q'), seg('sb_q'), seg('mla_kv'), seg('sb_k'), seg('sb_v'),
         zeros(KR_LANE), seg('mla_kr'), zeros(LANES - KR_LANE - MLA_ROPE), zeros(Z_W - Z_KR - LANES)], axis=-1)
    w_gate = seg('gate').reshape(depth, D_MODEL, N_BRANCH, D_MODEL).transpose(0, 2, 1, 3)
    pad_heads = lambda w, d: jnp.pad(w.reshape(depth, w.shape[1], MLA_HEADS, d),
                                     ((0, 0), (0, 0), (0, 0), (0, HEAD_PAD - d))).reshape(depth, w.shape[1], -1)
    place_kr = np.zeros((LANES, MLA_HEADS * HEAD_PAD), np.float32)
    sel_rope = np.zeros((LANES, LANES), np.float32)
    for i in range(MLA_ROPE):
        sel_rope[KR_LANE + i, i] = 1.0
        for h in range(MLA_HEADS):
            place_kr[KR_LANE + i, h * HEAD_PAD + MLA_NOPE + i] = 1.0
    wk = jnp.concatenate([pad_heads(mla_w_uk, MLA_NOPE),
                          jnp.broadcast_to(jnp.asarray(place_kr), (depth,) + place_kr.shape)], axis=1)
    wukt = mla_w_uk.transpose(0, 2, 1)
    wukt_pad = jnp.pad(wukt.reshape(depth, MLA_HEADS, MLA_NOPE, MLA_KV_LORA),
                       ((0, 0), (0, 0), (0, HEAD_PAD - MLA_NOPE), (0, 0)))
    wbd = jnp.einsum('lchd,hg->lhcgd', mla_w_uv.reshape(depth, MLA_KV_LORA, MLA_HEADS, MLA_V),
                     jnp.eye(MLA_HEADS, dtype=mla_w_uv.dtype)).reshape(depth, MLA_HEADS * MLA_KV_LORA, 512)
    place_sb = np.zeros((512, SB_HEADS * LANES), np.float32)
    for h in range(SB_HEADS):
        for d in range(SB_HEAD_DIM):
            place_sb[h * SB_HEAD_DIM + d, h * LANES + (h // SB_REP) * SB_HEAD_DIM + d] = 1.0
    pad_gain = lambda g, s: jnp.pad(g * s, ((0, 0), (0, HEAD_PAD - MLA_QK)))[:, None, :]
    tril = jnp.tril(jnp.ones((CHUNK, CHUNK), bool))
    pc = past_len % CHUNK
    wr = jnp.concatenate([router_expert_w, router_group_w,
                          jnp.zeros((depth, D_MODEL, LANES - N_EXPERTS - N_GROUPS), F32)], axis=-1)
    wr_hi = wr.astype(BF16)
    wr_lo = (wr - wr_hi.astype(F32)).astype(BF16)
    r_bias = jnp.concatenate([router_expert_b, router_group_b,
                              jnp.zeros((depth, LANES - N_EXPERTS - N_GROUPS), F32)], axis=-1)[:, None, :]
    row = lambda v: v[:, None, :]
    return dict(
        w_small=w_small.astype(BF16), w_gate=w_gate.astype(BF16), g_mix=row(norm_mix_g),
        g_qn=row(mla_q_norm_g), g_kvn=row(mla_kv_norm_g),
        wuq=pad_heads(mla_w_uq, MLA_QK).astype(BF16), wk=wk.astype(BF16),
        gq=pad_gain(mla_qk_norm_q, MLA_SCALE), gk=pad_gain(mla_qk_norm_k, 1.0),
        wukt=wukt.astype(BF16), wukt_pad=wukt_pad.astype(BF16), wbd=wbd.astype(BF16),
        sel_rope=jnp.asarray(sel_rope, BF16), place_sb=jnp.asarray(place_sb, BF16),
        g_ln_g=row(gmlp_ln_g), g_ln_b=row(gmlp_ln_b),
        g_wtril=jnp.where(tril, gmlp_ws, 0.0).astype(BF16),
        g_bias=jnp.broadcast_to(gmlp_b[:, :, :, None], gmlp_b.shape + (CHUNK,)),
        g_wdiag=row(jnp.repeat(gmlp_ws[:, :, pc, pc], CHUNK, axis=-1)),
        g_brow=row(jnp.repeat(gmlp_b[:, :, pc], CHUNK, axis=-1)),
        conv_w=conv_w, conv_b=row(conv_b), conv_ln_g=row(conv_ln_g), conv_ln_b=row(conv_ln_b),
        w_branch=w_branch.astype(BF16), w_out=w_out.astype(BF16), g_ffn=row(norm_ffn_g),
        wr_hi=wr_hi, wr_lo=wr_lo, r_bias=r_bias,
        wgu=jnp.concatenate([moe_w_gate, moe_w_up], axis=-1).astype(BF16), wd=moe_w_down.astype(BF16),
        g_ple=row(ple_norm_g), w_ple_gate=ple_w_gate.astype(BF16), w_ple_proj=ple_w_proj.astype(BF16),
        u512=_strict_upper(512), u256=_strict_upper(SB_BLK * CHUNK),
    )


def _rope_tables(pos):
    half = MLA_ROPE // 2
    inv = ROPE_BASE ** (-jnp.arange(half, dtype=F32) / half)
    ang = pos.astype(F32)[:, None] * inv[None, :]
    cos, sin = jnp.cos(ang), jnp.sin(ang)
    n = pos.shape[0]
    z = lambda w: jnp.zeros((n, w), F32)
    c_t = jnp.concatenate([jnp.ones((n, MLA_NOPE), F32), cos, cos, z(HEAD_PAD - MLA_QK)], axis=-1)
    sa_t = jnp.concatenate([z(MLA_NOPE + half), sin, z(HEAD_PAD - MLA_QK)], axis=-1)
    sb_t = jnp.concatenate([z(MLA_NOPE), -sin, z(half + HEAD_PAD - MLA_QK)], axis=-1)
    return c_t, sa_t, sb_t


def _tail(r, xn, branches, p_all, wts, layer, tm_mix, tm_out, tm_moe, tm_ple):
    mixed = gated_mix(xn, branches, wts, layer, tm_mix)
    r1, hn, comb = out_router(r, mixed, wts, layer, tm_out)
    ffn = moe_dense(hn, comb, wts, layer, tm_moe)
    return ple(r1, ffn, p_all, wts, layer, tm_ple)


def kernel(x_prompt, x_sample, cache_mla_ckv, cache_mla_kpe, cache_sb_k, cache_sb_v, state_conv, page_table, p_prompt, p_sample, norm_mix_g, w_in, conv_w, conv_b, conv_ln_g, conv_ln_b, mla_q_norm_g, mla_kv_norm_g, mla_w_uq, mla_w_uk, mla_w_uv, mla_qk_norm_q, mla_qk_norm_k, gmlp_ln_g, gmlp_ln_b, gmlp_ws, gmlp_b, w_branch, w_out, norm_ffn_g, router_group_w, router_group_b, router_expert_w, router_expert_b, moe_w_gate, moe_w_up, moe_w_down, ple_norm_g, ple_w_gate, ple_w_proj):
    n_batch, seq, d = x_prompt.shape
    nb, dec_seq, _ = x_sample.shape
    depth = w_in.shape[0]
    n_pages = page_table.shape[1]
    page = cache_mla_ckv.shape[2]
    past_len = n_pages * page
    assert d == D_MODEL and dec_seq == 1 and page == CHUNK
    assert seq % 512 == 0 and n_pages % MLA_PPS == 0
    assert n_pages >= SB_HEAD_PAGES and (n_pages - SB_HEAD_PAGES) % SB_BLK == 0

    wts = _prepare_weights(w_in, conv_w, conv_b, conv_ln_g, conv_ln_b, mla_q_norm_g, mla_kv_norm_g, mla_w_uq,
                           mla_w_uk, mla_w_uv, mla_qk_norm_q, mla_qk_norm_k, gmlp_ln_g, gmlp_ln_b, gmlp_ws, gmlp_b,
                           w_branch, w_out, norm_ffn_g, router_group_w, router_group_b, router_expert_w,
                           router_expert_b, moe_w_gate, moe_w_up, moe_w_down, ple_norm_g, ple_w_gate, ple_w_proj,
                           norm_mix_g, past_len)
    tabs_p = _rope_tables(jnp.tile(jnp.arange(seq), n_batch))
    tabs_s = _rope_tables(jnp.full((nb,), past_len, jnp.int32))
    page_flat = page_table.reshape(-1).astype(jnp.int32)
    n_pool = cache_sb_k.shape[1]
    cache_kt = cache_sb_k.transpose(0, 1, 3, 4, 2).reshape(depth, n_pool, LANES, page)
    cache_vt = cache_sb_v.transpose(0, 1, 3, 4, 2).reshape(depth, n_pool, LANES, page)
    cache_kpe_t = cache_mla_kpe.transpose(0, 1, 3, 2)
    state_t = state_conv.transpose(0, 2, 1, 3)
    p_p = p_prompt.reshape(depth, n_batch * seq, PLE_DIM)
    p_s = p_sample.reshape(depth, nb, PLE_DIM)

    rp = x_prompt.reshape(n_batch * seq, d)
    rs = x_sample.reshape(nb, d)
    st_p = [[] for _ in range(5)]
    st_s = [[] for _ in range(6)]
    for l in range(depth):
        zp, xnp = rms_matmul(rp, wts['g_mix'], wts['w_small'], l, 1024, 768)
        a_p, q_p, k_p, ckv_p, kpe_p, sbq_p, outd_p, _ = branch_prep(zp, tabs_p, wts, l, True, 256)
        outa_p = conv_prompt(a_p, wts, l, n_batch, seq)
        outb_p = mla_prompt(q_p, k_p, ckv_p, wts, l, n_batch, seq)
        outc_p = sb_prompt(sbq_p, zp, wts, n_batch, seq)
        rp = _tail(rp, xnp, (outa_p, outb_p, outc_p, outd_p), p_p, wts, l, 1024, 256, 1024, 256)
        st_p[0].append(a_p.reshape(n_batch, seq, 512)[:, seq - (CONV_WIDTH - 1):])
        st_p[1].append(ckv_p.reshape(n_batch, seq, MLA_KV_LORA))
        st_p[2].append(kpe_p[:, KR_LANE:KR_LANE + MLA_ROPE].reshape(n_batch, seq, MLA_ROPE))
        st_p[3].append(zp[:, Z_SK:Z_SK + 128].reshape(n_batch, seq, SB_KV_HEADS, SB_HEAD_DIM))
        st_p[4].append(zp[:, Z_SV:Z_SV + 128].reshape(n_batch, seq, SB_KV_HEADS, SB_HEAD_DIM))
        zs, xns = rms_matmul(rs, wts['g_mix'], wts['w_small'], l, nb, 768)
        a_s, q_s, k_s, ckv_s, kpe_s, sbq_s, outd_s, vln_s = branch_prep(zs, tabs_s, wts, l, False, nb)
        outa_s = conv_sample(state_t, a_s, wts, l)
        qabs, qrope = mla_sample_prep(q_s, wts, l)
        o_lat = mla_sample(q_s, k_s, qabs, qrope, ckv_s, cache_mla_ckv, cache_kpe_t, page_flat, wts, l, n_pages)
        outb_s = latent_value(o_lat.reshape(nb, MLA_HEADS * MLA_KV_LORA), wts, l)
        outc_s = sb_sample(sbq_s, cache_kt, cache_vt, page_flat, wts, l, n_pages)
        outc_s = outc_s.reshape(nb, 512).astype(BF16)
        rs = _tail(rs, xns, (outa_s, outb_s, outc_s, outd_s), p_s, wts, l, nb, nb, nb, nb)
        st_s[0].append(jnp.concatenate([state_conv[l][:, 1:], a_s[:, None, :]], axis=1))
        st_s[1].append(ckv_s.reshape(nb, 1, MLA_KV_LORA))
        st_s[2].append(kpe_s[:, KR_LANE:KR_LANE + MLA_ROPE].reshape(nb, 1, MLA_ROPE))
        st_s[3].append(zs[:, Z_SK:Z_SK + 128].reshape(nb, 1, SB_KV_HEADS, SB_HEAD_DIM))
        st_s[4].append(zs[:, Z_SV:Z_SV + 128].reshape(nb, 1, SB_KV_HEADS, SB_HEAD_DIM))
        st_s[5].append(vln_s.reshape(nb, 1, 512))
    conv_state_prompt, mla_ckv_prompt, mla_kpe_prompt, sb_k_prompt, sb_v_prompt = [jnp.stack(a) for a in st_p]
    conv_state_sample, mla_ckv_sample, mla_kpe_sample, sb_k_sample, sb_v_sample, gmlp_v_sample = [
        jnp.stack(a) for a in st_s]
    return (rp.reshape(n_batch, seq, d), rs.reshape(nb, 1, d), conv_state_prompt, conv_state_sample,
            mla_ckv_prompt, mla_ckv_sample, mla_kpe_prompt, mla_kpe_sample, sb_k_prompt, sb_k_sample,
            sb_v_prompt, sb_v_sample, gmlp_v_sample)
```

```python
import functools
import math

import numpy as np
import jax
import jax.numpy as jnp
from jax import lax
from jax.experimental import pallas as pl
from jax.experimental.pallas import tpu as pltpu

F32 = jnp.float32
BF16 = jnp.bfloat16

D_MODEL = 2048
N_BRANCH = 4
MIX_W = 512
PLE_DIM = 256
EPS = 1e-6
CONV_WIDTH = 31
MLA_HEADS = 8
MLA_NOPE = 64
MLA_ROPE = 32
MLA_QK = MLA_NOPE + MLA_ROPE
MLA_V = 64
MLA_Q_LORA = 512
MLA_KV_LORA = 256
MLA_SCALE = MLA_QK ** -0.5
ROPE_BASE = 10000.0
SB_HEADS = 8
SB_KV_HEADS = 2
SB_REP = SB_HEADS // SB_KV_HEADS
SB_HEAD_DIM = 64
SB_SCALE = SB_HEAD_DIM ** -0.5
GMLP_GROUPS = 4
CHUNK = 128
N_GROUPS = 4
EXPERTS_PER_GROUP = 8
N_EXPERTS = N_GROUPS * EXPERTS_PER_GROUP
D_EXPERT = 256

LANES = 128
HEAD_PAD = LANES
NEG_BIG = -1e30
SB_DEAD = -104.0

Z_CONV, Z_GMLP, Z_MQ, Z_SQ, Z_MKV, Z_SK, Z_SV, Z_KR = 0, 1024, 2048, 2560, 3072, 3328, 3456, 3584
Z_W = 3840
KR_LANE = 64

IN_SIZES = (('conv', 1024), ('mla_q', 512), ('mla_kv', 256), ('mla_kr', 32), ('sb_q', 512), ('sb_k', 128),
            ('sb_v', 128), ('gmlp', 1024), ('gate', N_BRANCH * D_MODEL))


def _cp(n_axes, vmem_mb=None):
    return pltpu.CompilerParams(
        dimension_semantics=("arbitrary",) * n_axes,
        vmem_limit_bytes=None if vmem_mb is None else vmem_mb * 2 ** 20)


def _dot(a, b):
    return jnp.dot(a, b, preferred_element_type=F32)


def _dot_nt(a, b):
    return lax.dot_general(a, b, (((1,), (1,)), ((), ())), preferred_element_type=F32)


def _rms_rows(x, g):
    return x * lax.rsqrt(jnp.mean(x * x, axis=-1, keepdims=True) + EPS) * g


def _ln_rows(x, g, b):
    xc = x - jnp.mean(x, axis=-1, keepdims=True)
    return xc * lax.rsqrt(jnp.mean(xc * xc, axis=-1, keepdims=True) + EPS) * g + b


def _logsig(z):
    return jnp.minimum(z, 0.0) - jnp.log1p(jnp.exp(-jnp.abs(z)))


def _split_bf16(x):
    hi = x.astype(BF16)
    lo = (x - hi.astype(F32)).astype(BF16)
    return hi, lo


def _rms_matmul_body(x_ref, g_ref, w_ref, z_ref, xn_ref):
    @pl.when(pl.program_id(1) == 0)
    def _():
        xn_ref[...] = _rms_rows(x_ref[...], g_ref[...]).astype(BF16)

    z_ref[...] = _dot(xn_ref[...], w_ref[...])


def rms_matmul(x, g, w_all, layer, tm, tn):
    m, k = x.shape
    n = w_all.shape[-1]
    return pl.pallas_call(
        _rms_matmul_body,
        grid=(m // tm, n // tn),
        in_specs=[pl.BlockSpec((tm, k), lambda i, j: (i, 0)),
                  pl.BlockSpec((None, 1, k), lambda i, j: (layer, 0, 0)),
                  pl.BlockSpec((None, k, tn), lambda i, j: (layer, 0, j))],
        out_specs=[pl.BlockSpec((tm, tn), lambda i, j: (i, j)),
                   pl.BlockSpec((tm, k), lambda i, j: (i, 0))],
        out_shape=[jax.ShapeDtypeStruct((m, n), F32), jax.ShapeDtypeStruct((m, k), BF16)],
        compiler_params=_cp(2, 48),
        name="rms_matmul",
    )(x, g, w_all)


def _rope_slab(x, c, sa, sb):
    return x * c + pltpu.roll(x, 16, 1) * sa + pltpu.roll(x, LANES - 16, 1) * sb


def _prep_body(z_ref, c_ref, sa_ref, sb_ref, gqn_ref, gkvn_ref, wuq_ref, wk_ref, gq_ref, gk_ref, place_ref,
               glg_ref, glb_ref, gw_ref, gb_ref,
               a_ref, q_ref, k_ref, ckv_ref, kpe_ref, sbq_ref, outd_ref, vln_ref, *, is_prompt, tm):
    c, sa, sb = c_ref[...], sa_ref[...], sb_ref[...]
    a_ref[...] = z_ref[:, Z_CONV:Z_CONV + 512] * jax.nn.sigmoid(z_ref[:, Z_CONV + 512:Z_CONV + 1024])
    cq = _rms_rows(z_ref[:, Z_MQ:Z_MQ + 512], gqn_ref[...]).astype(BF16)
    ckv = _rms_rows(z_ref[:, Z_MKV:Z_MKV + 256], gkvn_ref[...])
    ckv_ref[...] = ckv
    kpe = _rope_slab(z_ref[:, Z_KR:Z_KR + LANES], c, sa, sb)
    kpe_ref[...] = kpe
    q0 = _dot(cq, wuq_ref[...])
    kcat = jnp.concatenate([ckv, kpe], axis=-1).astype(BF16)
    k0 = _dot(kcat, wk_ref[...])
    gq, gk = gq_ref[...], gk_ref[...]
    for h in range(MLA_HEADS):
        sl = slice(h * HEAD_PAD, (h + 1) * HEAD_PAD)
        qh = _rope_slab(q0[:, sl], c, sa, sb)
        qh = qh * lax.rsqrt(jnp.sum(qh * qh, -1, keepdims=True) * (1.0 / MLA_QK) + EPS) * gq
        q_ref[:, sl] = qh.astype(BF16)
        kh = k0[:, sl]
        kh = kh * lax.rsqrt(jnp.sum(kh * kh, -1, keepdims=True) * (1.0 / MLA_QK) + EPS) * gk
        k_ref[:, sl] = kh.astype(BF16)
    sq = (z_ref[:, Z_SQ:Z_SQ + 512] * SB_SCALE).astype(BF16)
    sbq_ref[...] = _dot(sq, place_ref[...]).astype(BF16)
    ge = jax.nn.gelu(z_ref[:, Z_GMLP:Z_GMLP + 1024])
    u = ge[:, :512]
    vln = _ln_rows(ge[:, 512:], glg_ref[...], glb_ref[...])
    vln_ref[...] = vln
    if is_prompt:
        vb = vln.astype(BF16)
        for ch in range(tm // CHUNK):
            rs = slice(ch * CHUNK, (ch + 1) * CHUNK)
            for g in range(GMLP_GROUPS):
                cs = slice(g * 128, (g + 1) * 128)
                mix = _dot(gw_ref[g], vb[rs, cs]) + gb_ref[g]
                outd_ref[rs, cs] = (u[rs, cs] * mix).astype(BF16)
    else:
        outd_ref[...] = (u * (vln * gw_ref[...] + gb_ref[...])).astype(BF16)


def branch_prep(z, tabs, wts, layer, is_prompt, tm):
    m = z.shape[0]
    c_t, sa_t, sb_t = tabs
    row = lambda i: (i, 0)
    lay3 = lambda i: (layer, 0, 0)
    if is_prompt:
        gw, gb = wts['g_wtril'], wts['g_bias']
        gw_spec = pl.BlockSpec((None, GMLP_GROUPS, CHUNK, CHUNK), lambda i: (layer, 0, 0, 0))
        gb_spec = pl.BlockSpec((None, GMLP_GROUPS, CHUNK, CHUNK), lambda i: (layer, 0, 0, 0))
    else:
        gw, gb = wts['g_wdiag'], wts['g_brow']
        gw_spec = pl.BlockSpec((None, 1, 512), lay3)
        gb_spec = pl.BlockSpec((None, 1, 512), lay3)
    in_specs = [
        pl.BlockSpec((tm, Z_W), row),
        pl.BlockSpec((tm, LANES), row), pl.BlockSpec((tm, LANES), row), pl.BlockSpec((tm, LANES), row),
        pl.BlockSpec((None, 1, 512), lay3), pl.BlockSpec((None, 1, 256), lay3),
        pl.BlockSpec((None, 512, 1024), lay3), pl.BlockSpec((None, 384, 1024), lay3),
        pl.BlockSpec((None, 1, LANES), lay3), pl.BlockSpec((None, 1, LANES), lay3),
        pl.BlockSpec((512, 1024), lambda i: (0, 0)),
        pl.BlockSpec((None, 1, 512), lay3), pl.BlockSpec((None, 1, 512), lay3),
        gw_spec, gb_spec,
    ]
    outs = [(512, F32), (1024, BF16), (1024, BF16), (256, F32), (LANES, F32), (1024, BF16), (512, BF16), (512, F32)]
    return pl.pallas_call(
        functools.partial(_prep_body, is_prompt=is_prompt, tm=tm),
        grid=(m // tm,),
        in_specs=in_specs,
        out_specs=[pl.BlockSpec((tm, w), row) for w, _ in outs],
        out_shape=[jax.ShapeDtypeStruct((m, w), dt) for w, dt in outs],
        compiler_params=_cp(1, 48),
        name="branch_prep_p" if is_prompt else "branch_prep_s",
    )(z, c_t, sa_t, sb_t, wts['g_qn'], wts['g_kvn'], wts['wuq'], wts['wk'], wts['gq'], wts['gk'], wts['place_sb'],
      wts['g_ln_g'], wts['g_ln_b'], gw, gb)


CONV_HALO = 32


def _conv_p_body(a_ref, w_ref, b_ref, g_ref, bb_ref, o_ref, buf_ref, y_ref, *, tl):
    @pl.when(pl.program_id(1) == 0)
    def _():
        buf_ref[0:CONV_HALO, :] = jnp.zeros((CONV_HALO, 512), F32)

    buf_ref[CONV_HALO:CONV_HALO + tl, :] = a_ref[...]
    rb = 128
    shift = CONV_HALO - (CONV_WIDTH - 1)
    for r in range(tl // rb):
        for cb in range(4):
            cs = slice(cb * LANES, (cb + 1) * LANES)
            acc = jnp.zeros((rb, LANES), F32)
            for j in range(CONV_WIDTH):
                lo = r * rb + j + shift
                acc = acc + w_ref[j:j + 1, cs] * buf_ref[lo:lo + rb, cs]
            y_ref[r * rb:(r + 1) * rb, cs] = acc
    y = _ln_rows(y_ref[...] + b_ref[...], g_ref[...], bb_ref[...])
    o_ref[...] = jax.nn.silu(y).astype(BF16)
    buf_ref[0:CONV_HALO, :] = buf_ref[tl:tl + CONV_HALO, :]


def conv_prompt(a, wts, layer, n_batch, seq, tl=256):
    nt = seq // tl
    lay3 = lambda b, i: (layer, 0, 0)
    return pl.pallas_call(
        functools.partial(_conv_p_body, tl=tl),
        grid=(n_batch, nt),
        in_specs=[pl.BlockSpec((tl, 512), lambda b, i: (b * nt + i, 0)),
                  pl.BlockSpec((None, CONV_WIDTH, 512), lay3),
                  pl.BlockSpec((None, 1, 512), lay3), pl.BlockSpec((None, 1, 512), lay3),
                  pl.BlockSpec((None, 1, 512), lay3)],
        out_specs=pl.BlockSpec((tl, 512), lambda b, i: (b * nt + i, 0)),
        out_shape=jax.ShapeDtypeStruct((n_batch * seq, 512), BF16),
        scratch_shapes=[pltpu.VMEM((tl + CONV_HALO, 512), F32), pltpu.VMEM((tl, 512), F32)],
        compiler_params=_cp(2),
        name="conv_prompt",
    )(a, wts['conv_w'], wts['conv_b'], wts['conv_ln_g'], wts['conv_ln_b'])


def _conv_s_body(st_ref, a_ref, w_ref, b_ref, g_ref, bb_ref, o_ref):
    y = a_ref[...] * w_ref[CONV_WIDTH - 1:CONV_WIDTH, :] + b_ref[...]
    for j in range(CONV_WIDTH - 1):
        y = y + st_ref[j] * w_ref[j:j + 1, :]
    o_ref[...] = jax.nn.silu(_ln_rows(y, g_ref[...], bb_ref[...])).astype(BF16)


def conv_sample(state_t, a, wts, layer):
    nb = a.shape[0]
    lay3 = lambda i: (layer, 0, 0)
    return pl.pallas_call(
        _conv_s_body,
        grid=(1,),
        in_specs=[pl.BlockSpec((None, CONV_WIDTH - 1, nb, 512), lambda i: (layer, 0, 0, 0)),
                  pl.BlockSpec((nb, 512), lambda i: (0, 0)),
                  pl.BlockSpec((None, CONV_WIDTH, 512), lay3),
                  pl.BlockSpec((None, 1, 512), lay3), pl.BlockSpec((None, 1, 512), lay3),
                  pl.BlockSpec((None, 1, 512), lay3)],
        out_specs=pl.BlockSpec((nb, 512), lambda i: (0, 0)),
        out_shape=jax.ShapeDtypeStruct((nb, 512), BF16),
        compiler_params=_cp(1, 48),
        name="conv_sample",
    )(state_t, a, wts['conv_w'], wts['conv_b'], wts['conv_ln_g'], wts['conv_ln_b'])


def _pair_tables(n_tiles, descending):
    qs, ks = [], []
    for qi in range(n_tiles):
        kis = range(qi, -1, -1) if descending else range(qi + 1)
        for ki in kis:
            qs.append(qi)
            ks.append(ki)
    return np.asarray(qs, np.int32), np.asarray(ks, np.int32)


def _mla_flash_body(qt_ref, kt_ref, q_ref, k_ref, c_ref, wbd_ref, o_ref, m_ref, l_ref, acc_ref, *, tq, tk):
    s_idx = pl.program_id(1)
    qi, ki = qt_ref[s_idx], kt_ref[s_idx]

    @pl.when(ki == 0)
    def _():
        m_ref[...] = jnp.full(m_ref.shape, NEG_BIG, F32)
        l_ref[...] = jnp.zeros(l_ref.shape, F32)
        acc_ref[...] = jnp.zeros(acc_ref.shape, F32)

    def step(masked):
        c = c_ref[...].astype(BF16)
        if masked:
            allow = (lax.broadcasted_iota(jnp.int32, (tq, tk), 1) <= lax.broadcasted_iota(jnp.int32, (tq, tk), 0))
        for h in range(MLA_HEADS):
            sl = slice(h * HEAD_PAD, (h + 1) * HEAD_PAD)
            s = _dot_nt(q_ref[:, sl], k_ref[:, sl])
            if masked:
                s = jnp.where(allow, s, NEG_BIG)
            m_prev = m_ref[h]
            m_new = jnp.maximum(m_prev, jnp.max(s, -1, keepdims=True))
            alpha = jnp.exp(m_prev - m_new)
            p = jnp.exp(s - jnp.tile(m_new, (1, tk // LANES)))
            l_ref[h] = alpha * l_ref[h] + jnp.sum(p, -1, keepdims=True)
            acc_ref[h] = jnp.tile(alpha, (1, MLA_KV_LORA // LANES)) * acc_ref[h] + _dot(p.astype(BF16), c)
            m_ref[h] = m_new

    @pl.when(ki < qi)
    def _():
        step(False)

    @pl.when(ki == qi)
    def _():
        step(True)
        o = jnp.concatenate(
            [(acc_ref[h] * jnp.tile(1.0 / l_ref[h], (1, MLA_KV_LORA // LANES))).astype(BF16)
             for h in range(MLA_HEADS)], axis=-1)
        o_ref[...] = _dot(o, wbd_ref[...]).astype(BF16)


def mla_prompt(q, k, ckv, wts, layer, n_batch, seq, tq=512):
    nt = seq // tq
    qs, ks = _pair_tables(nt, descending=False)
    qmap = lambda b, s, qt, kt: (b * nt + qt[s], 0)
    kmap = lambda b, s, qt, kt: (b * nt + kt[s], 0)
    grid_spec = pltpu.PrefetchScalarGridSpec(
        num_scalar_prefetch=2,
        grid=(n_batch, len(qs)),
        in_specs=[pl.BlockSpec((tq, 1024), qmap), pl.BlockSpec((tq, 1024), kmap),
                  pl.BlockSpec((tq, MLA_KV_LORA), kmap),
                  pl.BlockSpec((None, MLA_HEADS * MLA_KV_LORA, 512), lambda b, s, qt, kt: (layer, 0, 0))],
        out_specs=pl.BlockSpec((tq, 512), qmap),
        scratch_shapes=[pltpu.VMEM((MLA_HEADS, tq, LANES), F32), pltpu.VMEM((MLA_HEADS, tq, LANES), F32),
                        pltpu.VMEM((MLA_HEADS, tq, MLA_KV_LORA), F32)],
    )
    return pl.pallas_call(
        functools.partial(_mla_flash_body, tq=tq, tk=tq),
        grid_spec=grid_spec,
        out_shape=jax.ShapeDtypeStruct((n_batch * seq, 512), BF16),
        compiler_params=_cp(2, 48),
        name="mla_prompt",
    )(jnp.asarray(qs), jnp.asarray(ks), q, k, ckv, wts['wbd'])


def _sb_tile(z, carry, u_mat, v_bf, allow, v_transposed=False):
    lb = _logsig(z)
    lk = lb - z
    if allow is not None:
        lk = jnp.where(allow, lk, 0.0)
    hi, lo = _split_bf16(lk)
    later = jnp.tile(carry, (1, z.shape[1] // LANES)) + _dot(hi, u_mat) + _dot(lo, u_mat)
    a = jnp.exp(lb + later)
    if allow is not None:
        a = jnp.where(allow, a, 0.0)
    pv = _dot_nt(a.astype(BF16), v_bf) if v_transposed else _dot(a.astype(BF16), v_bf)
    return pv, carry + jnp.sum(lk, -1, keepdims=True)


def _pick_group_half(acc_lo, acc_hi, group):
    lane = lax.broadcasted_iota(jnp.int32, acc_lo.shape, 1)
    if group == 0:
        return jnp.where(lane < 64, acc_lo, pltpu.roll(acc_hi, 64, 1))
    return jnp.where(lane < 64, pltpu.roll(acc_lo, 64, 1), acc_hi)


def _sb_flash_body(qt_ref, kt_ref, q_ref, k_ref, v_ref, u_ref, o_ref, carry_ref, acc_ref, alive_ref, *, tq, tk):
    s_idx = pl.program_id(1)
    qi, ki = qt_ref[s_idx], kt_ref[s_idx]

    @pl.when(ki == qi)
    def _():
        carry_ref[...] = jnp.zeros(carry_ref.shape, F32)
        acc_ref[...] = jnp.zeros(acc_ref.shape, F32)
        alive_ref[0] = 1

    def step(masked):
        k_bf = k_ref[...].astype(BF16)
        v_bf = v_ref[...].astype(BF16)
        u_mat = u_ref[...]
        allow = None
        if masked:
            allow = (lax.broadcasted_iota(jnp.int32, (tq, tk), 1) < lax.broadcasted_iota(jnp.int32, (tq, tk), 0))
        worst = None
        for h in range(SB_HEADS):
            z = _dot_nt(q_ref[:, h * LANES:(h + 1) * LANES], k_bf)
            pv, carry = _sb_tile(z, carry_ref[h], u_mat, v_bf, allow)
            acc_ref[h] = acc_ref[h] + pv
            carry_ref[h] = carry
            hmax = jnp.max(carry)
            worst = hmax if worst is None else jnp.maximum(worst, hmax)
        alive_ref[0] = (worst >= SB_DEAD).astype(jnp.int32)

    @pl.when(jnp.logical_and(ki == qi, alive_ref[0] == 1))
    def _():
        step(True)

    @pl.when(jnp.logical_and(ki < qi, alive_ref[0] == 1))
    def _():
        step(False)

    @pl.when(ki == 0)
    def _():
        for m in range(SB_HEADS // 2):
            blk = _pick_group_half(acc_ref[2 * m], acc_ref[2 * m + 1], (2 * m) // SB_REP)
            o_ref[:, m * LANES:(m + 1) * LANES] = blk.astype(BF16)


def sb_prompt(sbq, z, wts, n_batch, seq, tq=512):
    nt = seq // tq
    qs, ks = _pair_tables(nt, descending=True)
    kcol, vcol = Z_SK // LANES, Z_SV // LANES
    grid_spec = pltpu.PrefetchScalarGridSpec(
        num_scalar_prefetch=2,
        grid=(n_batch, len(qs)),
        in_specs=[pl.BlockSpec((tq, 1024), lambda b, s, qt, kt: (b * nt + qt[s], 0)),
                  pl.BlockSpec((tq, LANES), lambda b, s, qt, kt: (b * nt + kt[s], kcol)),
                  pl.BlockSpec((tq, LANES), lambda b, s, qt, kt: (b * nt + kt[s], vcol)),
                  pl.BlockSpec((tq, tq), lambda b, s, qt, kt: (0, 0))],
        out_specs=pl.BlockSpec((tq, 512), lambda b, s, qt, kt: (b * nt + qt[s], 0)),
        scratch_shapes=[pltpu.VMEM((SB_HEADS, tq, LANES), F32), pltpu.VMEM((SB_HEADS, tq, LANES), F32),
                        pltpu.SMEM((1,), jnp.int32)],
    )
    return pl.pallas_call(
        functools.partial(_sb_flash_body, tq=tq, tk=tq),
        grid_spec=grid_spec,
        out_shape=jax.ShapeDtypeStruct((n_batch * seq, 512), BF16),
        compiler_params=_cp(2, 48),
        name="sb_prompt",
    )(jnp.asarray(qs), jnp.asarray(ks), sbq, z, z, _strict_upper(tq))


def _mix_body(xn_ref, a_ref, b_ref, c_ref, d_ref, wg_ref, wb_ref, o_ref):
    xn = xn_ref[...]
    acc = None
    for i, br in enumerate((a_ref, b_ref, c_ref, d_ref)):
        term = jax.nn.sigmoid(_dot(xn, wg_ref[i])) * _dot(br[...], wb_ref[i])
        acc = term if acc is None else acc + term
    o_ref[...] = acc.astype(BF16)


def gated_mix(xn, branches, wts, layer, tm, tn=256):
    m = xn.shape[0]
    row = lambda i, j: (i, 0)
    return pl.pallas_call(
        _mix_body,
        grid=(m // tm, D_MODEL // tn),
        in_specs=[pl.BlockSpec((tm, D_MODEL), row)] + [pl.BlockSpec((tm, MIX_W), row)] * 4 + [
            pl.BlockSpec((None, N_BRANCH, D_MODEL, tn), lambda i, j: (layer, 0, 0, j)),
            pl.BlockSpec((None, N_BRANCH, MIX_W, tn), lambda i, j: (layer, 0, 0, j))],
        out_specs=pl.BlockSpec((tm, tn), lambda i, j: (i, j)),
        out_shape=jax.ShapeDtypeStruct((m, D_MODEL), BF16),
        compiler_params=_cp(2, 48),
        name="gated_mix",
    )(xn, *branches, wts['w_gate'], wts['w_branch'])


def _route(logits):
    lane = lax.broadcasted_iota(jnp.int32, logits.shape, 1)
    lane_f = lane.astype(F32)
    far = float(LANES)
    gmask = (lane >> 2) == (N_EXPERTS // 4)
    gl = jnp.where(gmask, logits, NEG_BIG)
    gmax = jnp.max(gl, -1, keepdims=True)
    g_p = 1.0 / jnp.sum(jnp.where(gmask, jnp.exp(gl - gmax), 0.0), -1, keepdims=True)
    gidx = jnp.min(jnp.where(gl == gmax, lane_f, far), -1, keepdims=True) - float(N_EXPERTS)
    el = jnp.where((lane >> 3).astype(F32) == gidx, logits, NEG_BIG)
    m1 = jnp.max(el, -1, keepdims=True)
    i1 = jnp.min(jnp.where(el == m1, lane_f, far), -1, keepdims=True)
    el2 = jnp.where(lane_f == i1, NEG_BIG, el)
    m2 = jnp.max(el2, -1, keepdims=True)
    i2 = jnp.min(jnp.where(el2 == m2, lane_f, far), -1, keepdims=True)
    e2 = jnp.exp(m2 - m1)
    inv = 1.0 / (1.0 + e2)
    return jnp.where(lane_f == i1, inv * g_p, 0.0) + jnp.where(lane_f == i2, e2 * inv * g_p, 0.0)


def _out_router_body(r_ref, mix_ref, wo_ref, g_ref, whi_ref, wlo_ref, rb_ref, r1_ref, hn_ref, comb_ref):
    r1 = r_ref[...] + _dot(mix_ref[...], wo_ref[...])
    r1_ref[...] = r1
    h = _rms_rows(r1, g_ref[...])
    hi, lo = _split_bf16(h)
    hn_ref[...] = hi
    logits = _dot(hi, whi_ref[...]) + _dot(lo, whi_ref[...]) + _dot(hi, wlo_ref[...]) + rb_ref[...]
    comb_ref[...] = _route(logits)


def out_router(r, mixed, wts, layer, tm):
    m = r.shape[0]
    row = lambda i: (i, 0)
    lay3 = lambda i: (layer, 0, 0)
    return pl.pallas_call(
        _out_router_body,
        grid=(m // tm,),
        in_specs=[pl.BlockSpec((tm, D_MODEL), row), pl.BlockSpec((tm, D_MODEL), row),
                  pl.BlockSpec((None, D_MODEL, D_MODEL), lay3), pl.BlockSpec((None, 1, D_MODEL), lay3),
                  pl.BlockSpec((None, D_MODEL, LANES), lay3), pl.BlockSpec((None, D_MODEL, LANES), lay3),
                  pl.BlockSpec((None, 1, LANES), lay3)],
        out_specs=[pl.BlockSpec((tm, D_MODEL), row), pl.BlockSpec((tm, D_MODEL), row),
                   pl.BlockSpec((tm, LANES), row)],
        out_shape=[jax.ShapeDtypeStruct((m, D_MODEL), F32), jax.ShapeDtypeStruct((m, D_MODEL), BF16),
                   jax.ShapeDtypeStruct((m, LANES), F32)],
        compiler_params=_cp(1, 48),
        name="out_router",
    )(r, mixed, wts['w_out'], wts['g_ffn'], wts['wr_hi'], wts['wr_lo'], wts['r_bias'])


def _moe_body(hn_ref, comb_ref, wgu_ref, wd_ref, o_ref):
    e = pl.program_id(1)

    @pl.when(e == 0)
    def _():
        o_ref[...] = jnp.zeros(o_ref.shape, F32)

    gu = _dot(hn_ref[...], wgu_ref[...])
    comb = comb_ref[...]
    lane = lax.broadcasted_iota(jnp.int32, comb.shape, 1)
    ce = jnp.sum(jnp.where(lane == e, comb, 0.0), -1, keepdims=True)
    hid = jax.nn.silu(gu[:, :D_EXPERT]) * gu[:, D_EXPERT:] * ce
    o_ref[...] += _dot(hid.astype(BF16), wd_ref[...])


def moe_dense(hn, comb, wts, layer, tm):
    m = hn.shape[0]
    row = lambda i, e: (i, 0)
    return pl.pallas_call(
        _moe_body,
        grid=(m // tm, N_EXPERTS),
        in_specs=[pl.BlockSpec((tm, D_MODEL), row), pl.BlockSpec((tm, LANES), row),
                  pl.BlockSpec((None, None, D_MODEL, 2 * D_EXPERT), lambda i, e: (layer, e, 0, 0)),
                  pl.BlockSpec((None, None, D_EXPERT, D_MODEL), lambda i, e: (layer, e, 0, 0))],
        out_specs=pl.BlockSpec((tm, D_MODEL), row),
        out_shape=jax.ShapeDtypeStruct((m, D_MODEL), F32),
        compiler_params=_cp(2, 48),
        name="moe_dense",
    )(hn, comb, wts['wgu'], wts['wd'])


def _ple_body(r_ref, f_ref, p_ref, g_ref, wg_ref, wp_ref, o_ref):
    r2 = r_ref[...] + f_ref[...]
    gate = jax.nn.sigmoid(_dot(_rms_rows(r2, g_ref[...]).astype(BF16), wg_ref[...]))
    o_ref[...] = r2 + gate * _dot(p_ref[...].astype(BF16), wp_ref[...])


def ple(r, ffn, p_all, wts, layer, tm):
    m = r.shape[0]
    row = lambda i: (i, 0)
    lay3 = lambda i: (layer, 0, 0)
    return pl.pallas_call(
        _ple_body,
        grid=(m // tm,),
        in_specs=[pl.BlockSpec((tm, D_MODEL), row), pl.BlockSpec((tm, D_MODEL), row),
                  pl.BlockSpec((None, tm, PLE_DIM), lambda i: (layer, i, 0)),
                  pl.BlockSpec((None, 1, D_MODEL), lay3),
                  pl.BlockSpec((None, D_MODEL, D_MODEL), lay3),
                  pl.BlockSpec((None, PLE_DIM, D_MODEL), lay3)],
        out_specs=pl.BlockSpec((tm, D_MODEL), row),
        out_shape=jax.ShapeDtypeStruct((m, D_MODEL), F32),
        compiler_params=_cp(1, 48),
        name="ple",
    )(r, ffn, p_all, wts['g_ple'], wts['w_ple_gate'], wts['w_ple_proj'])


def _mla_sprep_body(q_ref, gk_ref, wukt_ref, sel_ref, qabs_ref, qrope_ref):
    gk = gk_ref[...]
    for h in range(MLA_HEADS):
        sl = slice(h * HEAD_PAD, (h + 1) * HEAD_PAD)
        qg = (q_ref[:, sl].astype(F32) * gk).astype(BF16)
        qabs_ref[:, h * MLA_KV_LORA:(h + 1) * MLA_KV_LORA] = _dot(qg, wukt_ref[h]).astype(BF16)
        qrope_ref[:, sl] = _dot(qg, sel_ref[...]).astype(BF16)


def mla_sample_prep(q, wts, layer):
    nb = q.shape[0]
    return pl.pallas_call(
        _mla_sprep_body,
        grid=(1,),
        in_specs=[pl.BlockSpec((nb, 1024), lambda i: (0, 0)),
                  pl.BlockSpec((None, 1, LANES), lambda i: (layer, 0, 0)),
                  pl.BlockSpec((None, MLA_HEADS, HEAD_PAD, MLA_KV_LORA), lambda i: (layer, 0, 0, 0)),
                  pl.BlockSpec((LANES, LANES), lambda i: (0, 0))],
        out_specs=[pl.BlockSpec((nb, MLA_HEADS * MLA_KV_LORA), lambda i: (0, 0)),
                   pl.BlockSpec((nb, 1024), lambda i: (0, 0))],
        out_shape=[jax.ShapeDtypeStruct((nb, MLA_HEADS * MLA_KV_LORA), BF16),
                   jax.ShapeDtypeStruct((nb, 1024), BF16)],
        compiler_params=_cp(1),
        name="mla_sample_prep",
    )(q, wts['gk'], wts['wukt_pad'], wts['sel_rope'])


MLA_PPS = 64
MLA_SUB = 8


def _mla_dec_body(pt_ref, q8_ref, k8_ref, qabs_ref, qrope_ref, cnew_ref, wukt_ref, ckv_hbm, kpe_hbm, o_ref,
                  ckv_buf, kpe_buf, sems, m_ref, l_ref, acc_ref, *, layer, n_chunks):
    ci = pl.program_id(1)
    step = pl.program_id(0) * n_chunks + ci
    last_step = pl.num_programs(0) * n_chunks - 1
    slot = lax.rem(step, 2)

    def page_copies(s, slot_):
        cps = []
        for i in range(MLA_PPS):
            pid = pt_ref[s * MLA_PPS + i]
            cps.append(pltpu.make_async_copy(ckv_hbm.at[layer, pid], ckv_buf.at[slot_, i], sems.at[0, slot_]))
            cps.append(pltpu.make_async_copy(kpe_hbm.at[layer, pid], kpe_buf.at[slot_, i], sems.at[1, slot_]))
        return cps

    @pl.when(step == 0)
    def _():
        for cp in page_copies(0, 0):
            cp.start()

    @pl.when(ci == 0)
    def _():
        m_ref[...] = jnp.full(m_ref.shape, NEG_BIG, F32)
        l_ref[...] = jnp.zeros(l_ref.shape, F32)
        acc_ref[...] = jnp.zeros(acc_ref.shape, F32)

    for i in range(MLA_PPS):
        pltpu.make_async_copy(ckv_hbm.at[layer, 0], ckv_buf.at[slot, i], sems.at[0, slot]).wait()
        pltpu.make_async_copy(kpe_hbm.at[layer, 0], kpe_buf.at[slot, i], sems.at[1, slot]).wait()

    @pl.when(step < last_step)
    def _():
        for cp in page_copies(jnp.minimum(step + 1, last_step), 1 - slot):
            cp.start()

    qabs = qabs_ref[...]
    qrope = qrope_ref[:, :MLA_ROPE]
    wukt = wukt_ref[...]
    s_parts, c_parts = [], []
    for sub in range(MLA_PPS // MLA_SUB):
        pages = range(sub * MLA_SUB, (sub + 1) * MLA_SUB)
        c = jnp.concatenate([ckv_buf[slot, i] for i in pages], axis=0).astype(BF16)
        kpt = jnp.concatenate([kpe_buf[slot, i] for i in pages], axis=1)
        tk = c.shape[0]
        kt = _dot_nt(wukt, c)
        ssn = jnp.sum((kt * kt).reshape(MLA_HEADS, MLA_NOPE, tk), axis=1)
        ssr = jnp.sum(kpt * kpt, axis=0, keepdims=True)
        raw = _dot_nt(qabs, c) + _dot(qrope, kpt.astype(BF16))
        s_parts.append(raw * lax.rsqrt((ssn + ssr) * (1.0 / MLA_QK) + EPS))
        c_parts.append(c)
    s = jnp.concatenate(s_parts, axis=1)
    c_all = jnp.concatenate(c_parts, axis=0)
    m_prev = m_ref[...]
    m_new = jnp.maximum(m_prev, jnp.max(s, -1, keepdims=True))
    alpha = jnp.exp(m_prev - m_new)
    p = jnp.exp(s - jnp.tile(m_new, (1, s.shape[1] // LANES)))
    l_ref[...] = alpha * l_ref[...] + jnp.sum(p, -1, keepdims=True)
    acc_ref[...] = jnp.tile(alpha, (1, MLA_KV_LORA // LANES)) * acc_ref[...] + _dot(p.astype(BF16), c_all)
    m_ref[...] = m_new

    @pl.when(ci == n_chunks - 1)
    def _():
        s_self = jnp.sum(q8_ref[...].astype(F32) * k8_ref[...].astype(F32), -1, keepdims=True)
        m_prev = m_ref[...]
        m_fin = jnp.maximum(m_prev, s_self)
        alpha = jnp.exp(m_prev - m_fin)
        p_self = jnp.exp(s_self - m_fin[:, :1])
        l_fin = alpha[:, :1] * l_ref[:, :1] + p_self
        acc = alpha[:, :1] * acc_ref[...] + p_self * cnew_ref[...]
        o_ref[...] = acc * (1.0 / l_fin)


def mla_sample(q, k, qabs, qrope, ckv_new, cache_ckv, cache_kpe_t, page_flat, wts, layer, n_pages):
    nb = q.shape[0]
    n_chunks = n_pages // MLA_PPS
    page = cache_ckv.shape[2]
    seq3 = lambda b, c, pt: (b, 0, 0)
    assert n_pages == n_chunks * MLA_PPS

    grid_spec = pltpu.PrefetchScalarGridSpec(
        num_scalar_prefetch=1,
        grid=(nb, n_chunks),
        in_specs=[pl.BlockSpec((None, MLA_HEADS, HEAD_PAD), seq3), pl.BlockSpec((None, MLA_HEADS, HEAD_PAD), seq3),
                  pl.BlockSpec((None, MLA_HEADS, MLA_KV_LORA), seq3), pl.BlockSpec((None, MLA_HEADS, HEAD_PAD), seq3),
                  pl.BlockSpec((None, 1, MLA_KV_LORA), seq3),
                  pl.BlockSpec((None, MLA_HEADS * MLA_NOPE, MLA_KV_LORA), lambda b, c, pt: (layer, 0, 0)),
                  pl.BlockSpec(memory_space=pl.ANY), pl.BlockSpec(memory_space=pl.ANY)],
        out_specs=pl.BlockSpec((None, MLA_HEADS, MLA_KV_LORA), seq3),
        scratch_shapes=[pltpu.VMEM((2, MLA_PPS, page, MLA_KV_LORA), F32), pltpu.VMEM((2, MLA_PPS, MLA_ROPE, page), F32),
                        pltpu.SemaphoreType.DMA((2, 2)),
                        pltpu.VMEM((MLA_HEADS, LANES), F32), pltpu.VMEM((MLA_HEADS, LANES), F32),
                        pltpu.VMEM((MLA_HEADS, MLA_KV_LORA), F32)],
    )
    return pl.pallas_call(
        functools.partial(_mla_dec_body, layer=layer, n_chunks=n_chunks),
        grid_spec=grid_spec,
        out_shape=jax.ShapeDtypeStruct((nb, MLA_HEADS, MLA_KV_LORA), F32),
        compiler_params=_cp(2, 48),
        name="mla_sample",
    )(page_flat, q.reshape(nb, MLA_HEADS, HEAD_PAD), k.reshape(nb, MLA_HEADS, HEAD_PAD),
      qabs.reshape(nb, MLA_HEADS, MLA_KV_LORA), qrope.reshape(nb, MLA_HEADS, HEAD_PAD),
      ckv_new.reshape(nb, 1, MLA_KV_LORA), wts['wukt'], cache_ckv, cache_kpe_t)


def _latent_value_body(o_ref, wbd_ref, out_ref):
    out_ref[...] = _dot(o_ref[...].astype(BF16), wbd_ref[...]).astype(BF16)


def latent_value(o_lat, wts, layer):
    nb = o_lat.shape[0]
    return pl.pallas_call(
        _latent_value_body,
        grid=(1,),
        in_specs=[pl.BlockSpec((nb, MLA_HEADS * MLA_KV_LORA), lambda i: (0, 0)),
                  pl.BlockSpec((None, MLA_HEADS * MLA_KV_LORA, 512), lambda i: (layer, 0, 0))],
        out_specs=pl.BlockSpec((nb, 512), lambda i: (0, 0)),
        out_shape=jax.ShapeDtypeStruct((nb, 512), BF16),
        compiler_params=_cp(1),
        name="latent_value",
    )(o_lat, wts['wbd'])


SB_HEAD_PAGES = 4
SB_BLK = 2


def _sb_dec_body(pt_ref, q_ref, u_ref, kt_hbm, vt_hbm, *rest, layer, n_pages):
    k_refs = rest[:SB_HEAD_PAGES]
    v_refs = rest[SB_HEAD_PAGES:2 * SB_HEAD_PAGES]
    o_ref, kbuf, vbuf, sems, carry_ref, acc_ref, alive_ref = rest[2 * SB_HEAD_PAGES:]
    b = pl.program_id(0)
    q = q_ref[...]
    u_mat = u_ref[...]

    def block(kt, vt, carry, acc):
        pv, carry = _sb_tile(_dot(q, kt.astype(BF16)), carry, u_mat, vt.astype(BF16), None, v_transposed=True)
        return carry, acc + pv

    def publish(carry, acc):
        carry_ref[...] = carry
        acc_ref[...] = acc
        alive_ref[0] = (jnp.max(carry) >= SB_DEAD).astype(jnp.int32)

    carry = jnp.zeros((SB_HEADS, LANES), F32)
    acc = jnp.zeros((SB_HEADS, LANES), F32)
    for blk in range(SB_HEAD_PAGES // SB_BLK - 1, -1, -1):
        pages = range(blk * SB_BLK, (blk + 1) * SB_BLK)
        carry, acc = block(jnp.concatenate([k_refs[i][...] for i in pages], axis=1),
                           jnp.concatenate([v_refs[i][...] for i in pages], axis=1), carry, acc)
    publish(carry, acc)

    @pl.when(alive_ref[0] == 1)
    def _():
        def older(i, _):
            @pl.when(alive_ref[0] == 1)
            def _():
                first = n_pages - SB_HEAD_PAGES - SB_BLK * (i + 1)
                copies = []
                for j in range(SB_BLK):
                    pid = pt_ref[b * n_pages + first + j]
                    copies.append(pltpu.make_async_copy(kt_hbm.at[layer, pid], kbuf.at[j], sems.at[0, j]))
                    copies.append(pltpu.make_async_copy(vt_hbm.at[layer, pid], vbuf.at[j], sems.at[1, j]))
                for cp in copies:
                    cp.start()
                for cp in copies:
                    cp.wait()
                carry, acc = block(jnp.concatenate([kbuf[j] for j in range(SB_BLK)], axis=1),
                                   jnp.concatenate([vbuf[j] for j in range(SB_BLK)], axis=1),
                                   carry_ref[...], acc_ref[...])
                publish(carry, acc)
            return 0

        lax.fori_loop(0, (n_pages - SB_HEAD_PAGES) // SB_BLK, older, 0)

    acc = acc_ref[...]
    row = lax.broadcasted_iota(jnp.int32, acc.shape, 0)
    o_ref[...] = jnp.where(row < SB_REP, acc, pltpu.roll(acc, 64, 1))[:, :SB_HEAD_DIM]


def sb_sample(sbq, cache_kt, cache_vt, page_flat, wts, layer, n_pages):
    nb = sbq.shape[0]
    page = cache_kt.shape[3]
    head_map = lambda i: (lambda b, pt: (layer, pt[b * n_pages + n_pages - SB_HEAD_PAGES + i], 0, 0))
    head_specs = [pl.BlockSpec((None, None, LANES, page), head_map(i)) for i in range(SB_HEAD_PAGES)]
    grid_spec = pltpu.PrefetchScalarGridSpec(
        num_scalar_prefetch=1,
        grid=(nb,),
        in_specs=[pl.BlockSpec((None, SB_HEADS, LANES), lambda b, pt: (b, 0, 0)),
                  pl.BlockSpec((SB_BLK * page, SB_BLK * page), lambda b, pt: (0, 0)),
                  pl.BlockSpec(memory_space=pl.ANY), pl.BlockSpec(memory_space=pl.ANY)]
        + head_specs * 2,
        out_specs=pl.BlockSpec((None, SB_HEADS, SB_HEAD_DIM), lambda b, pt: (b, 0, 0)),
        scratch_shapes=[pltpu.VMEM((SB_BLK, LANES, page), F32), pltpu.VMEM((SB_BLK, LANES, page), F32),
                        pltpu.SemaphoreType.DMA((2, SB_BLK)),
                        pltpu.VMEM((SB_HEADS, LANES), F32), pltpu.VMEM((SB_HEADS, LANES), F32),
                        pltpu.SMEM((1,), jnp.int32)],
    )
    return pl.pallas_call(
        functools.partial(_sb_dec_body, layer=layer, n_pages=n_pages),
        grid_spec=grid_spec,
        out_shape=jax.ShapeDtypeStruct((nb, SB_HEADS, SB_HEAD_DIM), F32),
        compiler_params=_cp(1),
        name="sb_sample",
    )(page_flat, sbq.reshape(nb, SB_HEADS, LANES), wts['u256'], cache_kt, cache_vt,
      *([cache_kt] * SB_HEAD_PAGES), *([cache_vt] * SB_HEAD_PAGES))


def _strict_upper(n):
    return jnp.asarray(np.tril(np.ones((n, n), np.float32), -1), BF16)


def _prepare_weights(w_in, conv_w, conv_b, conv_ln_g, conv_ln_b, mla_q_norm_g, mla_kv_norm_g, mla_w_uq, mla_w_uk,
                     mla_w_uv, mla_qk_norm_q, mla_qk_norm_k, gmlp_ln_g, gmlp_ln_b, gmlp_ws, gmlp_b, w_branch, w_out,
                     norm_ffn_g, router_group_w, router_group_b, router_expert_w, router_expert_b, moe_w_gate,
                     moe_w_up, moe_w_down, ple_norm_g, ple_w_gate, ple_w_proj, norm_mix_g, past_len):
    depth = w_in.shape[0]
    off, o = {}, 0
    for name, n in IN_SIZES:
        off[name] = (o, o + n)
        o += n
    seg = lambda name: w_in[:, :, off[name][0]:off[name][1]]
    zeros = lambda n: jnp.zeros((depth, D_MODEL, n), w_in.dtype)
    w_small = jnp.concatenate(
        [seg('conv'), seg('gmlp'), seg('mla_q'), seg('sb_q'), seg('mla_kv'), seg('sb_k'), seg('sb_v'),
         zeros(KR_LANE), seg('mla_kr'), zeros(LANES - KR_LANE - MLA_ROPE), zeros(Z_W - Z_KR - LANES)], axis=-1)
    w_gate = seg('gate').reshape(depth, D_MODEL, N_BRANCH, D_MODEL).transpose(0, 2, 1, 3)
    pad_heads = lambda w, d: jnp.pad(w.reshape(depth, w.shape[1], MLA_HEADS, d),
                                     ((0, 0), (0, 0), (0, 0), (0, HEAD_PAD - d))).reshape(depth, w.shape[1], -1)
    place_kr = np.zeros((LANES, MLA_HEADS * HEAD_PAD), np.float32)
    sel_rope = np.zeros((LANES, LANES), np.float32)
    for i in range(MLA_ROPE):
        sel_rope[KR_LANE + i, i] = 1.0
        for h in range(MLA_HEADS):
            place_kr[KR_LANE + i, h * HEAD_PAD + MLA_NOPE + i] = 1.0
    wk = jnp.concatenate([pad_heads(mla_w_uk, MLA_NOPE),
                          jnp.broadcast_to(jnp.asarray(place_kr), (depth,) + place_kr.shape)], axis=1)
    wukt = mla_w_uk.transpose(0, 2, 1)
    wukt_pad = jnp.pad(wukt.reshape(depth, MLA_HEADS, MLA_NOPE, MLA_KV_LORA),
                       ((0, 0), (0, 0), (0, HEAD_PAD - MLA_NOPE), (0, 0)))
    wbd = jnp.einsum('lchd,hg->lhcgd', mla_w_uv.reshape(depth, MLA_KV_LORA, MLA_HEADS, MLA_V),
                     jnp.eye(MLA_HEADS, dtype=mla_w_uv.dtype)).reshape(depth, MLA_HEADS * MLA_KV_LORA, 512)
    place_sb = np.zeros((512, SB_HEADS * LANES), np.float32)
    for h in range(SB_HEADS):
        for d in range(SB_HEAD_DIM):
            place_sb[h * SB_HEAD_DIM + d, h * LANES + (h // SB_REP) * SB_HEAD_DIM + d] = 1.0
    pad_gain = lambda g, s: jnp.pad(g * s, ((0, 0), (0, HEAD_PAD - MLA_QK)))[:, None, :]
    tril = jnp.tril(jnp.ones((CHUNK, CHUNK), bool))
    pc = past_len % CHUNK
    wr = jnp.concatenate([router_expert_w, router_group_w,
                          jnp.zeros((depth, D_MODEL, LANES - N_EXPERTS - N_GROUPS), F32)], axis=-1)
    wr_hi = wr.astype(BF16)
    wr_lo = (wr - wr_hi.astype(F32)).astype(BF16)
    r_bias = jnp.concatenate([router_expert_b, router_group_b,
                              jnp.zeros((depth, LANES - N_EXPERTS - N_GROUPS), F32)], axis=-1)[:, None, :]
    row = lambda v: v[:, None, :]
    return dict(
        w_small=w_small.astype(BF16), w_gate=w_gate.astype(BF16), g_mix=row(norm_mix_g),
        g_qn=row(mla_q_norm_g), g_kvn=row(mla_kv_norm_g),
        wuq=pad_heads(mla_w_uq, MLA_QK).astype(BF16), wk=wk.astype(BF16),
        gq=pad_gain(mla_qk_norm_q, MLA_SCALE), gk=pad_gain(mla_qk_norm_k, 1.0),
        wukt=wukt.astype(BF16), wukt_pad=wukt_pad.astype(BF16), wbd=wbd.astype(BF16),
        sel_rope=jnp.asarray(sel_rope, BF16), place_sb=jnp.asarray(place_sb, BF16),
        g_ln_g=row(gmlp_ln_g), g_ln_b=row(gmlp_ln_b),
        g_wtril=jnp.where(tril, gmlp_ws, 0.0).astype(BF16),
        g_bias=jnp.broadcast_to(gmlp_b[:, :, :, None], gmlp_b.shape + (CHUNK,)),
        g_wdiag=row(jnp.repeat(gmlp_ws[:, :, pc, pc], CHUNK, axis=-1)),
        g_brow=row(jnp.repeat(gmlp_b[:, :, pc], CHUNK, axis=-1)),
        conv_w=conv_w, conv_b=row(conv_b), conv_ln_g=row(conv_ln_g), conv_ln_b=row(conv_ln_b),
        w_branch=w_branch.astype(BF16), w_out=w_out.astype(BF16), g_ffn=row(norm_ffn_g),
        wr_hi=wr_hi, wr_lo=wr_lo, r_bias=r_bias,
        wgu=jnp.concatenate([moe_w_gate, moe_w_up], axis=-1).astype(BF16), wd=moe_w_down.astype(BF16),
        g_ple=row(ple_norm_g), w_ple_gate=ple_w_gate.astype(BF16), w_ple_proj=ple_w_proj.astype(BF16),
        u512=_strict_upper(512), u256=_strict_upper(SB_BLK * CHUNK),
    )


def _rope_tables(pos):
    half = MLA_ROPE // 2
    inv = ROPE_BASE ** (-jnp.arange(half, dtype=F32) / half)
    ang = pos.astype(F32)[:, None] * inv[None, :]
    cos, sin = jnp.cos(ang), jnp.sin(ang)
    n = pos.shape[0]
    z = lambda w: jnp.zeros((n, w), F32)
    c_t = jnp.concatenate([jnp.ones((n, MLA_NOPE), F32), cos, cos, z(HEAD_PAD - MLA_QK)], axis=-1)
    sa_t = jnp.concatenate([z(MLA_NOPE + half), sin, z(HEAD_PAD - MLA_QK)], axis=-1)
    sb_t = jnp.concatenate([z(MLA_NOPE), -sin, z(half + HEAD_PAD - MLA_QK)], axis=-1)
    return c_t, sa_t, sb_t


def _tail(r, xn, branches, p_all, wts, layer, tm_mix, tm_out, tm_moe, tm_ple):
    mixed = gated_mix(xn, branches, wts, layer, tm_mix)
    r1, hn, comb = out_router(r, mixed, wts, layer, tm_out)
    ffn = moe_dense(hn, comb, wts, layer, tm_moe)
    return ple(r1, ffn, p_all, wts, layer, tm_ple)


def kernel(x_prompt, x_sample, cache_mla_ckv, cache_mla_kpe, cache_sb_k, cache_sb_v, state_conv, page_table, p_prompt, p_sample, norm_mix_g, w_in, conv_w, conv_b, conv_ln_g, conv_ln_b, mla_q_norm_g, mla_kv_norm_g, mla_w_uq, mla_w_uk, mla_w_uv, mla_qk_norm_q, mla_qk_norm_k, gmlp_ln_g, gmlp_ln_b, gmlp_ws, gmlp_b, w_branch, w_out, norm_ffn_g, router_group_w, router_group_b, router_expert_w, router_expert_b, moe_w_gate, moe_w_up, moe_w_down, ple_norm_g, ple_w_gate, ple_w_proj):
    n_batch, seq, d = x_prompt.shape
    nb, dec_seq, _ = x_sample.shape
    depth = w_in.shape[0]
    n_pages = page_table.shape[1]
    page = cache_mla_ckv.shape[2]
    past_len = n_pages * page
    assert d == D_MODEL and dec_seq == 1 and page == CHUNK
    assert seq % 512 == 0 and n_pages % MLA_PPS == 0
    assert n_pages >= SB_HEAD_PAGES and (n_pages - SB_HEAD_PAGES) % SB_BLK == 0

    wts = _prepare_weights(w_in, conv_w, conv_b, conv_ln_g, conv_ln_b, mla_q_norm_g, mla_kv_norm_g, mla_w_uq,
                           mla_w_uk, mla_w_uv, mla_qk_norm_q, mla_qk_norm_k, gmlp_ln_g, gmlp_ln_b, gmlp_ws, gmlp_b,
                           w_branch, w_out, norm_ffn_g, router_group_w, router_group_b, router_expert_w,
                           router_expert_b, moe_w_gate, moe_w_up, moe_w_down, ple_norm_g, ple_w_gate, ple_w_proj,
                           norm_mix_g, past_len)
    tabs_p = _rope_tables(jnp.tile(jnp.arange(seq), n_batch))
    tabs_s = _rope_tables(jnp.full((nb,), past_len, jnp.int32))
    page_flat = page_table.reshape(-1).astype(jnp.int32)
    n_pool = cache_sb_k.shape[1]
    cache_kt = cache_sb_k.transpose(0, 1, 3, 4, 2).reshape(depth, n_pool, LANES, page)
    cache_vt = cache_sb_v.transpose(0, 1, 3, 4, 2).reshape(depth, n_pool, LANES, page)
    cache_kpe_t = cache_mla_kpe.transpose(0, 1, 3, 2)
    state_t = state_conv.transpose(0, 2, 1, 3)
    p_p = p_prompt.reshape(depth, n_batch * seq, PLE_DIM)
    p_s = p_sample.reshape(depth, nb, PLE_DIM)

    rp = x_prompt.reshape(n_batch * seq, d)
    rs = x_sample.reshape(nb, d)
    st_p = [[] for _ in range(5)]
    st_s = [[] for _ in range(6)]
    for l in range(depth):
        zp, xnp = rms_matmul(rp, wts['g_mix'], wts['w_small'], l, 1024, 768)
        a_p, q_p, k_p, ckv_p, kpe_p, sbq_p, outd_p, _ = branch_prep(zp, tabs_p, wts, l, True, 256)
        outa_p = conv_prompt(a_p, wts, l, n_batch, seq)
        outb_p = mla_prompt(q_p, k_p, ckv_p, wts, l, n_batch, seq)
        outc_p = sb_prompt(sbq_p, zp, wts, n_batch, seq)
        rp = _tail(rp, xnp, (outa_p, outb_p, outc_p, outd_p), p_p, wts, l, 1024, 256, 1024, 256)
        st_p[0].append(a_p.reshape(n_batch, seq, 512)[:, seq - (CONV_WIDTH - 1):])
        st_p[1].append(ckv_p.reshape(n_batch, seq, MLA_KV_LORA))
        st_p[2].append(kpe_p[:, KR_LANE:KR_LANE + MLA_ROPE].reshape(n_batch, seq, MLA_ROPE))
        st_p[3].append(zp[:, Z_SK:Z_SK + 128].reshape(n_batch, seq, SB_KV_HEADS, SB_HEAD_DIM))
        st_p[4].append(zp[:, Z_SV:Z_SV + 128].reshape(n_batch, seq, SB_KV_HEADS, SB_HEAD_DIM))
        zs, xns = rms_matmul(rs, wts['g_mix'], wts['w_small'], l, nb, 768)
        a_s, q_s, k_s, ckv_s, kpe_s, sbq_s, outd_s, vln_s = branch_prep(zs, tabs_s, wts, l, False, nb)
        outa_s = conv_sample(state_t, a_s, wts, l)
        qabs, qrope = mla_sample_prep(q_s, wts, l)
        o_lat = mla_sample(q_s, k_s, qabs, qrope, ckv_s, cache_mla_ckv, cache_kpe_t, page_flat, wts, l, n_pages)
        outb_s = latent_value(o_lat.reshape(nb, MLA_HEADS * MLA_KV_LORA), wts, l)
        outc_s = sb_sample(sbq_s, cache_kt, cache_vt, page_flat, wts, l, n_pages)
        outc_s = outc_s.reshape(nb, 512).astype(BF16)
        rs = _tail(rs, xns, (outa_s, outb_s, outc_s, outd_s), p_s, wts, l, nb, nb, nb, nb)
        st_s[0].append(jnp.concatenate([state_conv[l][:, 1:], a_s[:, None, :]], axis=1))
        st_s[1].append(ckv_s.reshape(nb, 1, MLA_KV_LORA))
        st_s[2].append(kpe_s[:, KR_LANE:KR_LANE + MLA_ROPE].reshape(nb, 1, MLA_ROPE))
        st_s[3].append(zs[:, Z_SK:Z_SK + 128].reshape(nb, 1, SB_KV_HEADS, SB_HEAD_DIM))
        st_s[4].append(zs[:, Z_SV:Z_SV + 128].reshape(nb, 1, SB_KV_HEADS, SB_HEAD_DIM))
        st_s[5].append(vln_s.reshape(nb, 1, 512))
    conv_state_prompt, mla_ckv_prompt, mla_kpe_prompt, sb_k_prompt, sb_v_prompt = [jnp.stack(a) for a in st_p]
    conv_state_sample, mla_ckv_sample, mla_kpe_sample, sb_k_sample, sb_v_sample, gmlp_v_sample = [
        jnp.stack(a) for a in st_s]
    return (rp.reshape(n_batch, seq, d), rs.reshape(nb, 1, d), conv_state_prompt, conv_state_sample,
            mla_ckv_prompt, mla_ckv_sample, mla_kpe_prompt, mla_kpe_sample, sb_k_prompt, sb_k_sample,
            sb_v_prompt, sb_v_sample, gmlp_v_sample)
```
